```python
import jax, jax.numpy as jnp
from jax import lax
import numpy as np

D_MODEL = 1024
BATCH = 8
SEQ = 2048
DEPTH = 2
DEC_BATCH = 128
DEC_SEQ = 4
PAST_LEN = 16384
PAGE_SIZE = 128

N_A_LAYERS = DEPTH // 2
N_B_LAYERS = DEPTH - N_A_LAYERS
CONV_WIDTH = 31
D_FF = 4 * D_MODEL
N_HEADS = 8
QK_NOPE_DIM = 128
QK_ROPE_DIM = 64
V_HEAD_DIM = 128
Q_LORA = 768
KV_LORA = 256
ROPE_BASE = 10000.0
Q_BLOCK = 128
RMS_EPS = 1e-6
LN_EPS = 1e-5
ATTN_SCALE = (QK_NOPE_DIM + QK_ROPE_DIM) ** -0.5

kernel_name = 'yoco_conformer_conv_mla_decoder_step'


def rmsnorm(x, g):
    xf = x.astype(jnp.float32)
    y = xf * lax.rsqrt(jnp.mean(xf * xf, axis=-1, keepdims=True) + RMS_EPS)
    return (y * g.astype(jnp.float32)).astype(x.dtype)


def layernorm(x, g, b):
    xf = x.astype(jnp.float32)
    mu = jnp.mean(xf, axis=-1, keepdims=True)
    xc = xf - mu
    var = jnp.mean(xc * xc, axis=-1, keepdims=True)
    return (xc * lax.rsqrt(var + LN_EPS) * g.astype(jnp.float32) + b.astype(jnp.float32)).astype(x.dtype)


def rope_tables(pos):
    inv = 1.0 / (ROPE_BASE ** (jnp.arange(0, QK_ROPE_DIM, 2, dtype=jnp.float32) / QK_ROPE_DIM))
    ang = pos.astype(jnp.float32)[:, None] * inv[None, :]
    return jnp.cos(ang), jnp.sin(ang)


def apply_rope(x, cos, sin):
    xf = x.astype(jnp.float32)
    half = QK_ROPE_DIM // 2
    x1, x2 = xf[..., :half], xf[..., half:]
    return jnp.concatenate([x1 * cos - x2 * sin, x2 * cos + x1 * sin], axis=-1).astype(x.dtype)


def conv_module(h, hist, w_pw1, b_pw1, w_dw, b_dw, g_ln, b_ln, w_pw2, b_pw2):
    a = h @ w_pw1 + b_pw1
    u = a[..., :D_MODEL] * jax.nn.sigmoid(a[..., D_MODEL:])
    ext = jnp.concatenate([hist.astype(u.dtype), u], axis=1)
    v = lax.conv_general_dilated(ext, w_dw[:, None, :], (1,), 'VALID',
                                 dimension_numbers=('NWC', 'WIO', 'NWC'),
                                 feature_group_count=D_MODEL) + b_dw
    v = layernorm(v, g_ln, b_ln)
    y = jax.nn.silu(v) @ w_pw2 + b_pw2
    return y, ext[:, -(CONV_WIDTH - 1):]


def squared_relu_mlp(h, w_up, w_down):
    return jnp.square(jax.nn.relu(h @ w_up)) @ w_down


def shared_kv(x, cos, sin, g_kv_in, w_dkv, g_ckv, w_kr):
    hn = rmsnorm(x, g_kv_in)
    ckv = rmsnorm(hn @ w_dkv, g_ckv)
    kpe = apply_rope(hn @ w_kr, cos, sin)
    return ckv, kpe


def mla_queries(h, cos, sin, w_dq, g_q, w_uq, w_qr, w_uk):
    n, t, _ = h.shape
    cq = rmsnorm(h @ w_dq, g_q)
    q_nope = (cq @ w_uq).reshape(n, t, N_HEADS, QK_NOPE_DIM)
    q_pe = apply_rope((cq @ w_qr).reshape(n, t, N_HEADS, QK_ROPE_DIM), cos[:, None, :], sin[:, None, :])
    q_lat = jnp.einsum('nthd,lhd->nthl', q_nope, w_uk)
    return q_lat, q_pe


def prompt_attention(q_lat, q_pe, ckv, kpe):
    n, s, h, l = q_lat.shape
    nb = s // Q_BLOCK
    ql = q_lat.reshape(n, nb, Q_BLOCK, h, l).transpose(1, 0, 2, 3, 4)
    qp = q_pe.reshape(n, nb, Q_BLOCK, h, QK_ROPE_DIM).transpose(1, 0, 2, 3, 4)
    k_pos = jnp.arange(s)

    def block(args):
        qlb, qpb, start = args
        sc = (jnp.einsum('nqhl,nkl->nhqk', qlb, ckv) + jnp.einsum('nqhr,nkr->nhqk', qpb, kpe)).astype(jnp.float32) * ATTN_SCALE
        q_pos = start + jnp.arange(Q_BLOCK)
        mask = k_pos[None, :] <= q_pos[:, None]
        p = jax.nn.softmax(jnp.where(mask, sc, -jnp.inf), axis=-1).astype(ckv.dtype)
        return jnp.einsum('nhqk,nkl->nqhl', p, ckv)

    out = lax.map(block, (ql, qp, jnp.arange(nb) * Q_BLOCK))
    return out.transpose(1, 0, 2, 3, 4).reshape(n, s, h, l)


def sample_attention(q_lat, q_pe, ckv_new, kpe_new, cache_ckv, cache_kpe, page_table):
    t = q_lat.shape[1]
    causal = jnp.arange(t)[None, :] <= jnp.arange(t)[:, None]

    def one(args):
        ql, qp, cn, kn, pages = args
        cp = cache_ckv[pages].reshape(-1, KV_LORA)
        kp = cache_kpe[pages].reshape(-1, QK_ROPE_DIM)
        n_past = cp.shape[0]
        s_past = (jnp.einsum('thl,pl->htp', ql, cp) + jnp.einsum('thr,pr->htp', qp, kp)).astype(jnp.float32) * ATTN_SCALE
        s_new = (jnp.einsum('thl,sl->hts', ql, cn) + jnp.einsum('thr,sr->hts', qp, kn)).astype(jnp.float32) * ATTN_SCALE
        s_new = jnp.where(causal, s_new, -jnp.inf)
        p = jax.nn.softmax(jnp.concatenate([s_past, s_new], axis=-1), axis=-1).astype(cp.dtype)
        return jnp.einsum('htp,pl->thl', p[..., :n_past], cp) + jnp.einsum('hts,sl->thl', p[..., n_past:], cn)

    return lax.map(one, (q_lat, q_pe, ckv_new, kpe_new, page_table))


def setup_inputs(seed: int = 0) -> dict:
    key = jax.random.key(seed)
    ks = iter(jax.random.split(key, 40))
    f32 = jnp.float32
    n_pages = PAST_LEN // PAGE_SIZE
    n_used = DEC_BATCH * n_pages
    n_phys = (n_used * 5) // 4

    def nrm(shape, scale):
        return jax.random.normal(next(ks), shape, f32) * scale

    def gain(shape):
        return 1.0 + nrm(shape, 0.05)

    na, nbl = N_A_LAYERS, N_B_LAYERS
    d = D_MODEL
    page_table = jax.random.permutation(next(ks), n_phys)[:n_used].reshape(DEC_BATCH, n_pages).astype(jnp.int32)
    return {
        'x_prompt': nrm((BATCH, SEQ, d), 1.0),
        'x_sample': nrm((DEC_BATCH, DEC_SEQ, d), 1.0),
        'state_conv': nrm((na, DEC_BATCH, CONV_WIDTH - 1, d), 0.5),
        'cache_ckv': nrm((n_phys, PAGE_SIZE, KV_LORA), 1.0),
        'cache_kpe': nrm((n_phys, PAGE_SIZE, QK_ROPE_DIM), 1.0),
        'page_table': page_table,
        'g_mix': gain((DEPTH, d)),
        'g_ffn': gain((DEPTH, d)),
        'w_up': nrm((DEPTH, d, D_FF), d ** -0.5),
        'w_down': nrm((DEPTH, D_FF, d), 0.5 * D_FF ** -0.5),
        'w_pw1': nrm((na, d, 2 * d), d ** -0.5),
        'b_pw1': nrm((na, 2 * d), 0.02),
        'w_dw': nrm((na, CONV_WIDTH, d), CONV_WIDTH ** -0.5),
        'b_dw': nrm((na, d), 0.02),
        'g_cln': gain((na, d)),
        'b_cln': nrm((na, d), 0.02),
        'w_pw2': nrm((na, d, d), d ** -0.5),
        'b_pw2': nrm((na, d), 0.02),
        'g_kv_in': gain((d,)),
        'w_dkv': nrm((d, KV_LORA), d ** -0.5),
        'g_ckv': gain((KV_LORA,)),
        'w_kr': nrm((d, QK_ROPE_DIM), d ** -0.5),
        'w_uk': nrm((KV_LORA, N_HEADS, QK_NOPE_DIM), KV_LORA ** -0.5),
        'w_uv': nrm((KV_LORA, N_HEADS, V_HEAD_DIM), KV_LORA ** -0.5),
        'w_dq': nrm((nbl, d, Q_LORA), d ** -0.5),
        'g_q': gain((nbl, Q_LORA)),
        'w_uq': nrm((nbl, Q_LORA, N_HEADS * QK_NOPE_DIM), Q_LORA ** -0.5),
        'w_qr': nrm((nbl, Q_LORA, N_HEADS * QK_ROPE_DIM), Q_LORA ** -0.5),
        'w_o': nrm((nbl, N_HEADS * V_HEAD_DIM, d), (N_HEADS * V_HEAD_DIM) ** -0.5),
        'g_final': gain((d,)),
    }


def reference(x_prompt, x_sample, state_conv, cache_ckv, cache_kpe, page_table,
              g_mix, g_ffn, w_up, w_down,
              w_pw1, b_pw1, w_dw, b_dw, g_cln, b_cln, w_pw2, b_pw2,
              g_kv_in, w_dkv, g_ckv, w_kr, w_uk, w_uv,
              w_dq, g_q, w_uq, w_qr, w_o, g_final):

    def run(x, hist, cos, sin, attend):
        n, t, _ = x.shape
        new_hist = []
        ckv = None
        kpe = None
        for l in range(DEPTH):
            h = rmsnorm(x, g_mix[l])
            if l < N_A_LAYERS:
                y, nh = conv_module(h, hist[l], w_pw1[l], b_pw1[l], w_dw[l], b_dw[l],
                                    g_cln[l], b_cln[l], w_pw2[l], b_pw2[l])
                new_hist.append(nh)
            else:
                j = l - N_A_LAYERS
                q_lat, q_pe = mla_queries(h, cos, sin, w_dq[j], g_q[j], w_uq[j], w_qr[j], w_uk)
                o_lat = attend(q_lat, q_pe, ckv, kpe)
                o = jnp.einsum('nthl,lhv->nthv', o_lat, w_uv).reshape(n, t, N_HEADS * V_HEAD_DIM)
                y = o @ w_o[j]
            x = x + y
            x = x + squared_relu_mlp(rmsnorm(x, g_ffn[l]), w_up[l], w_down[l])
            if l == N_A_LAYERS - 1:
                ckv, kpe = shared_kv(x, cos, sin, g_kv_in, w_dkv, g_ckv, w_kr)
        return rmsnorm(x, g_final), jnp.stack(new_hist, axis=0), ckv, kpe

    s = x_prompt.shape[1]
    cos_p, sin_p = rope_tables(jnp.arange(s))
    hist_p = jnp.zeros((N_A_LAYERS, x_prompt.shape[0], CONV_WIDTH - 1, D_MODEL), x_prompt.dtype)
    y_prompt, conv_p, ckv_p, kpe_p = run(x_prompt, hist_p, cos_p, sin_p, prompt_attention)

    past_len = page_table.shape[1] * cache_ckv.shape[1]
    cos_s, sin_s = rope_tables(past_len + jnp.arange(x_sample.shape[1]))

    def attend_sample(q_lat, q_pe, ckv_new, kpe_new):
        return sample_attention(q_lat, q_pe, ckv_new, kpe_new, cache_ckv, cache_kpe, page_table)

    y_sample, conv_s, ckv_s, kpe_s = run(x_sample, state_conv, cos_s, sin_s, attend_sample)

    return (y_prompt, y_sample, conv_p, ckv_p, kpe_p, conv_s, ckv_s, kpe_s)
```

```python
import functools

import jax
import jax.numpy as jnp
from jax import lax
from jax.experimental import pallas as pl
from jax.experimental.pallas import tpu as pltpu

D_MODEL = 1024
D_FF = 4 * D_MODEL
CONV_WIDTH = 31
N_HEADS = 8
QK_NOPE_DIM = 128
QK_ROPE_DIM = 64
V_HEAD_DIM = 128
Q_LORA = 768
KV_LORA = 256
KV_DIM = KV_LORA + QK_ROPE_DIM
ROPE_BASE = 10000.0
RMS_EPS = 1e-6
LN_EPS = 1e-5
ATTN_SCALE = (QK_NOPE_DIM + QK_ROPE_DIM) ** -0.5

LANES = 128
SUBLANES = 8
VMEM_LIMIT_BYTES = 56 * 1024 * 1024

ROW_TILE = 512
FF_TILE = 1024
CONV_TAIL = 32
CONV_ROW_CHUNK = 64
ATTN_TILE = 256
PAGES_PER_STEP = 16

F32 = jnp.float32
BF16 = jnp.bfloat16


def _params(*semantics):
    return pltpu.CompilerParams(dimension_semantics=semantics,
                                vmem_limit_bytes=VMEM_LIMIT_BYTES)


def _rms(x, g):
    return x * lax.rsqrt(jnp.mean(x * x, axis=-1, keepdims=True) + RMS_EPS) * g


def _dot(a, b):
    return jnp.dot(a, b, preferred_element_type=F32)


def _dot_nt(a, b):
    return lax.dot_general(a, b, (((1,), (1,)), ((), ())), preferred_element_type=F32)


def _full(shape):
    return pl.BlockSpec(shape, lambda *_: (0,) * len(shape))


def _pw1_glu_kernel(x_ref, g_ref, wa_ref, wb_ref, ba_ref, bb_ref, u_ref):
    hn = _rms(x_ref[...], g_ref[...]).astype(BF16)
    a = _dot(hn, wa_ref[...]) + ba_ref[...]
    b = _dot(hn, wb_ref[...]) + bb_ref[...]
    u_ref[...] = a * jax.nn.sigmoid(b)


def _pw1_glu(x, g, wa, wb, ba, bb):
    m, d = x.shape
    tm = min(ROW_TILE, m)
    row = pl.BlockSpec((tm, d), lambda i: (i, 0))
    return pl.pallas_call(
        _pw1_glu_kernel,
        grid=(m // tm,),
        in_specs=[row, _full((1, d)), _full((d, d)), _full((d, d)), _full((1, d)), _full((1, d))],
        out_specs=row,
        out_shape=jax.ShapeDtypeStruct((m, d), F32),
        compiler_params=_params("arbitrary"),
        name="pw1_glu",
    )(x, g, wa, wb, ba, bb)


def _conv_kernel(u_ref, hist_ref, x_ref, wdw_ref, bdw_ref, gln_ref, bln_ref, w2_ref, b2_ref,
                 o_ref, win_ref, v_ref, slab_ref):
    bn, tt, d = u_ref.shape
    rc = min(CONV_ROW_CHUNK, tt)
    n_row_chunks = tt // rc

    @pl.when(pl.program_id(1) == 0)
    def _():
        win_ref[:, 0:CONV_TAIL, :] = hist_ref[...]

    win_ref[:, CONV_TAIL:CONV_TAIL + tt, :] = u_ref[...]

    lead = CONV_TAIL - (CONV_WIDTH - 1)

    def chunk(idx, carry):
        b = idx // n_row_chunks
        r0 = pl.multiple_of((idx % n_row_chunks) * rc, SUBLANES)
        slab_ref[...] = win_ref[b, pl.ds(r0, rc + CONV_TAIL), :]
        for c in range(d // LANES):
            cols = slice(c * LANES, (c + 1) * LANES)
            acc = jnp.zeros((rc, LANES), F32)
            for k in range(CONV_WIDTH):
                acc = acc + wdw_ref[k:k + 1, cols] * slab_ref[lead + k:lead + k + rc, cols]
            v_ref[b, pl.ds(r0, rc), cols] = acc
        return carry

    lax.fori_loop(0, bn * n_row_chunks, chunk, 0)

    win_ref[:, 0:CONV_TAIL, :] = win_ref[:, tt:tt + CONV_TAIL, :]

    v = v_ref[...].reshape(bn * tt, d) + bdw_ref[...]
    mu = jnp.mean(v, axis=-1, keepdims=True)
    vc = v - mu
    var = jnp.mean(vc * vc, axis=-1, keepdims=True)
    vn = vc * lax.rsqrt(var + LN_EPS) * gln_ref[...] + bln_ref[...]
    s = vn * jax.nn.sigmoid(vn)
    y = _dot(s.astype(BF16), w2_ref[...]) + b2_ref[...]
    o_ref[...] = x_ref[...] + y.reshape(bn, tt, d)


def _conv_tail(u, hist, x, wdw, bdw, gln, bln, w2, b2, bn, tt):
    n, t, d = u.shape
    blk = pl.BlockSpec((bn, tt, d), lambda i, j: (i, j, 0))
    return pl.pallas_call(
        _conv_kernel,
        grid=(n // bn, t // tt),
        in_specs=[blk,
                  pl.BlockSpec((bn, CONV_TAIL, d), lambda i, j: (i, 0, 0)),
                  blk,
                  _full((CONV_WIDTH, d)), _full((1, d)), _full((1, d)), _full((1, d)),
                  _full((d, d)), _full((1, d))],
        out_specs=blk,
        out_shape=jax.ShapeDtypeStruct((n, t, d), F32),
        scratch_shapes=[pltpu.VMEM((bn, CONV_TAIL + tt, d), F32),
                        pltpu.VMEM((bn, tt, d), F32),
                        pltpu.VMEM((min(CONV_ROW_CHUNK, tt) + CONV_TAIL, d), F32)],
        compiler_params=_params("arbitrary", "arbitrary"),
        name="conv_tail",
    )(u, hist, x, wdw, bdw, gln, bln, w2, b2)


def _mlp_kernel(final_norm, x_ref, g_ref, wup_ref, wdn_ref, gf_ref, o_ref, hn_ref, acc_ref):
    f = pl.program_id(1)

    @pl.when(f == 0)
    def _():
        x = x_ref[...]
        hn_ref[...] = _rms(x, g_ref[...]).astype(BF16)
        acc_ref[...] = x

    h = _dot(hn_ref[...], wup_ref[...])
    h = jnp.square(jnp.maximum(h, 0.0)).astype(BF16)
    acc_ref[...] += _dot(h, wdn_ref[...])

    @pl.when(f == pl.num_programs(1) - 1)
    def _():
        out = acc_ref[...]
        if final_norm:
            out = _rms(out, gf_ref[...])
        o_ref[...] = out


def _mlp(x, g, wup, wdn, gf, final_norm):
    m, d = x.shape
    tm = min(ROW_TILE, m)
    row = pl.BlockSpec((tm, d), lambda i, f: (i, 0))
    return pl.pallas_call(
        functools.partial(_mlp_kernel, final_norm),
        grid=(m // tm, D_FF // FF_TILE),
        in_specs=[row, _full((1, d)),
                  pl.BlockSpec((d, FF_TILE), lambda i, f: (0, f)),
                  pl.BlockSpec((FF_TILE, d), lambda i, f: (f, 0)),
                  _full((1, d))],
        out_specs=row,
        out_shape=jax.ShapeDtypeStruct((m, d), F32),
        scratch_shapes=[pltpu.VMEM((tm, d), BF16), pltpu.VMEM((tm, d), F32)],
        compiler_params=_params("arbitrary", "arbitrary"),
        name="mlp_final" if final_norm else "mlp",
    )(x, g, wup, wdn, gf)


KV_W_COLS = 4 * LANES


def _shared_kv_kernel(x_ref, g_ref, w_ref, gc_ref, cos_ref, sin_ref, ckv_ref, kpe_ref, kvb_ref):
    hn = _rms(x_ref[...], g_ref[...]).astype(BF16)
    a = _dot(hn, w_ref[...])
    ckv = _rms(a[:, :KV_LORA], gc_ref[...])
    kr = a[:, KV_LORA:KV_LORA + QK_ROPE_DIM]
    ks = a[:, KV_LORA + LANES:KV_LORA + LANES + QK_ROPE_DIM]
    kpe = kr * cos_ref[...] + ks * sin_ref[...]
    ckv_ref[...] = ckv
    kpe_ref[...] = kpe
    kvb_ref[:, :KV_LORA] = ckv.astype(BF16)
    kvb_ref[:, KV_LORA:] = kpe.astype(BF16)


def _shared_kv(x, g, w, gc, cos, sin):
    m, d = x.shape
    tm = min(ROW_TILE, m)
    n_pos = cos.shape[0] // tm
    row = lambda w_: pl.BlockSpec((tm, w_), lambda i: (i, 0))
    pos = pl.BlockSpec((tm, QK_ROPE_DIM), lambda i: (i % n_pos, 0))
    return pl.pallas_call(
        _shared_kv_kernel,
        grid=(m // tm,),
        in_specs=[row(d), _full((1, d)), _full((d, KV_W_COLS)), _full((1, KV_LORA)), pos, pos],
        out_specs=[row(KV_LORA), row(QK_ROPE_DIM), row(KV_DIM)],
        out_shape=[jax.ShapeDtypeStruct((m, KV_LORA), F32),
                   jax.ShapeDtypeStruct((m, QK_ROPE_DIM), F32),
                   jax.ShapeDtypeStruct((m, KV_DIM), BF16)],
        compiler_params=_params("arbitrary"),
        name="shared_kv",
    )(x, g, w, gc, cos, sin)


def _mla_q_kernel(x_ref, g_ref, wdq_ref, gq_ref, wuq_ref, wqr_ref, wqs_ref, wukt_ref,
                  cos_ref, sin_ref, q_ref):
    h = _rms(x_ref[...], g_ref[...]).astype(BF16)
    cq = _rms(_dot(h, wdq_ref[...]), gq_ref[...]).astype(BF16)
    qn = _dot(cq, wuq_ref[...]).astype(BF16)
    qpe = _dot(cq, wqr_ref[...]) * cos_ref[...] + _dot(cq, wqs_ref[...]) * sin_ref[...]
    for hd in range(N_HEADS):
        q_lat = _dot(qn[:, hd * QK_NOPE_DIM:(hd + 1) * QK_NOPE_DIM], wukt_ref[hd])
        q_ref[hd, :, :KV_LORA] = q_lat.astype(BF16)
        q_ref[hd, :, KV_LORA:] = qpe[:, hd * QK_ROPE_DIM:(hd + 1) * QK_ROPE_DIM].astype(BF16)


def _mla_q(x, g, wdq, gq, wuq, wqr, wqs, wukt, cos, sin):
    m, d = x.shape
    tm = min(ROW_TILE, m)
    n_pos = cos.shape[0] // tm
    rope_w = N_HEADS * QK_ROPE_DIM
    pos = pl.BlockSpec((tm, rope_w), lambda i: (i % n_pos, 0))
    return pl.pallas_call(
        _mla_q_kernel,
        grid=(m // tm,),
        in_specs=[pl.BlockSpec((tm, d), lambda i: (i, 0)), _full((1, d)),
                  _full((d, Q_LORA)), _full((1, Q_LORA)),
                  _full((Q_LORA, N_HEADS * QK_NOPE_DIM)),
                  _full((Q_LORA, rope_w)), _full((Q_LORA, rope_w)),
                  _full((N_HEADS, QK_NOPE_DIM, KV_LORA)), pos, pos],
        out_specs=pl.BlockSpec((N_HEADS, tm, KV_DIM), lambda i: (0, i, 0)),
        out_shape=jax.ShapeDtypeStruct((N_HEADS, m, KV_DIM), BF16),
        compiler_params=_params("arbitrary"),
        name="mla_q",
    )(x, g, wdq, gq, wuq, wqr, wqs, wukt, cos, sin)


def _softmax_block(s, kv_lat, m_ref, l_ref, acc_ref, first):
    m_cur = jnp.max(s, axis=-1, keepdims=True)
    if first:
        m_new = jnp.broadcast_to(m_cur, m_ref.shape)
    else:
        m_prev = m_ref[...]
        m_new = jnp.maximum(m_prev, m_cur)
    p = jnp.exp(s - m_new[:, :1])
    l_cur = jnp.sum(p, axis=-1, keepdims=True)
    pv = _dot(p.astype(BF16), kv_lat)
    if first:
        l_ref[...] = jnp.broadcast_to(l_cur, l_ref.shape)
        acc_ref[...] = pv
    else:
        alpha = jnp.exp(m_prev - m_new)
        l_ref[...] = alpha * l_ref[...] + l_cur
        acc_ref[...] = alpha[:, :1] * acc_ref[...] + pv
    m_ref[...] = m_new


def _prompt_attn_kernel(q_ref, kv_ref, o_ref, m_ref, l_ref, acc_ref):
    n_heads, tq, _ = q_ref.shape
    rows = n_heads * tq
    qi = pl.program_id(1)
    q = q_ref[...].reshape(rows, KV_DIM)

    k0 = pl.multiple_of(qi * tq, tq)
    kv = kv_ref[pl.ds(k0, tq), :]
    s = _dot_nt(q, kv) * ATTN_SCALE
    t_idx = lax.broadcasted_iota(jnp.int32, (n_heads, tq, tq), 1).reshape(rows, tq)
    c_idx = lax.broadcasted_iota(jnp.int32, (rows, tq), 1)
    s = jnp.where(c_idx <= t_idx, s, -jnp.inf)
    _softmax_block(s, kv[:, :KV_LORA], m_ref, l_ref, acc_ref, first=True)

    def body(j, carry):
        kj = pl.multiple_of(j * tq, tq)
        kvj = kv_ref[pl.ds(kj, tq), :]
        sj = _dot_nt(q, kvj) * ATTN_SCALE
        _softmax_block(sj, kvj[:, :KV_LORA], m_ref, l_ref, acc_ref, first=False)
        return carry

    lax.fori_loop(0, qi, body, 0)

    o = acc_ref[...] / l_ref[:, :1]
    o_ref[...] = o.reshape(n_heads, tq, KV_LORA).astype(BF16)


def _prompt_attn(q, kvb, n, t):
    tq = ATTN_TILE
    nq = t // tq
    rows = N_HEADS * tq
    return pl.pallas_call(
        _prompt_attn_kernel,
        grid=(n, nq),
        in_specs=[pl.BlockSpec((N_HEADS, tq, KV_DIM), lambda b, i: (0, b * nq + i, 0)),
                  pl.BlockSpec((t, KV_DIM), lambda b, i: (b, 0))],
        out_specs=pl.BlockSpec((N_HEADS, tq, KV_LORA), lambda b, i: (0, b * nq + i, 0)),
        out_shape=jax.ShapeDtypeStruct((N_HEADS, n * t, KV_LORA), BF16),
        scratch_shapes=[pltpu.VMEM((rows, LANES), F32), pltpu.VMEM((rows, LANES), F32),
                        pltpu.VMEM((rows, KV_LORA), F32)],
        compiler_params=_params("arbitrary", "arbitrary"),
        name="prompt_attn",
    )(q, kvb)


NEW_ROWS_PAD = 16


def _sample_attn_kernel(pt_ref, q_ref, new_ref, *refs):
    g = PAGES_PER_STEP
    ckv_refs, kpe_refs = refs[:g], refs[g:2 * g]
    o_ref, kl_ref, kp_ref, m_ref, l_ref, acc_ref = refs[2 * g:]
    page = ckv_refs[0].shape[1]
    rows = q_ref.shape[1]
    n_new = rows // N_HEADS
    step = pl.program_id(1)
    q = q_ref[0]

    @pl.when(step == 0)
    def _():
        kn = new_ref[0]
        s = _dot_nt(q, kn) * ATTN_SCALE
        t_idx = lax.broadcasted_iota(jnp.int32, s.shape, 0) % n_new
        c_idx = lax.broadcasted_iota(jnp.int32, s.shape, 1)
        s = jnp.where(c_idx <= t_idx, s, -jnp.inf)
        _softmax_block(s, kn[:, :KV_LORA], m_ref, l_ref, acc_ref, first=True)

    for i in range(g):
        kl_ref[i * page:(i + 1) * page, :] = ckv_refs[i][0].astype(BF16)
        kp_ref[i * page:(i + 1) * page, :] = kpe_refs[i][0].astype(BF16)
    kl = kl_ref[...]
    s = (_dot_nt(q[:, :KV_LORA], kl) + _dot_nt(q[:, KV_LORA:], kp_ref[...])) * ATTN_SCALE
    _softmax_block(s, kl, m_ref, l_ref, acc_ref, first=False)

    @pl.when(step == pl.num_programs(1) - 1)
    def _():
        o_ref[0] = (acc_ref[...] / l_ref[:, :1]).astype(BF16)


def _sample_attn(page_table, q, kv_new, cache_ckv, cache_kpe):
    n_seq, n_pages = page_table.shape
    g = PAGES_PER_STEP
    page = cache_ckv.shape[1]
    rows = q.shape[1]

    def page_spec(width, i):
        return pl.BlockSpec((1, page, width),
                            lambda b, s, pt: (pt[b * n_pages + s * g + i], 0, 0))

    grid_spec = pltpu.PrefetchScalarGridSpec(
        num_scalar_prefetch=1,
        grid=(n_seq, n_pages // g),
        in_specs=([pl.BlockSpec((1, rows, KV_DIM), lambda b, s, pt: (b, 0, 0)),
                   pl.BlockSpec((1, NEW_ROWS_PAD, KV_DIM), lambda b, s, pt: (b, 0, 0))]
                  + [page_spec(KV_LORA, i) for i in range(g)]
                  + [page_spec(QK_ROPE_DIM, i) for i in range(g)]),
        out_specs=pl.BlockSpec((1, rows, KV_LORA), lambda b, s, pt: (b, 0, 0)),
        scratch_shapes=[pltpu.VMEM((g * page, KV_LORA), BF16),
                        pltpu.VMEM((g * page, QK_ROPE_DIM), BF16),
                        pltpu.VMEM((rows, LANES), F32), pltpu.VMEM((rows, LANES), F32),
                        pltpu.VMEM((rows, KV_LORA), F32)])
    return pl.pallas_call(
        _sample_attn_kernel,
        grid_spec=grid_spec,
        out_shape=jax.ShapeDtypeStruct((n_seq, rows, KV_LORA), BF16),
        compiler_params=_params("arbitrary", "arbitrary"),
        name="sample_attn",
    )(page_table.reshape(-1), q, kv_new, *([cache_ckv] * g), *([cache_kpe] * g))


def _attn_out_kernel(o_ref, x_ref, wuv_ref, wo_ref, y_ref):
    parts = [_dot(o_ref[hd], wuv_ref[hd]) for hd in range(N_HEADS)]
    o = jnp.concatenate(parts, axis=-1).astype(BF16)
    y_ref[...] = x_ref[...] + _dot(o, wo_ref[...])


def _attn_out(o_lat, x, wuv, wo):
    m, d = x.shape
    tm = min(ROW_TILE, m)
    row = pl.BlockSpec((tm, d), lambda i: (i, 0))
    return pl.pallas_call(
        _attn_out_kernel,
        grid=(m // tm,),
        in_specs=[pl.BlockSpec((N_HEADS, tm, KV_LORA), lambda i: (0, i, 0)), row,
                  _full((N_HEADS, KV_LORA, V_HEAD_DIM)),
                  _full((N_HEADS * V_HEAD_DIM, d))],
        out_specs=row,
        out_shape=jax.ShapeDtypeStruct((m, d), F32),
        compiler_params=_params("arbitrary"),
        name="attn_out",
    )(o_lat, x, wuv, wo)


def _rope_tables(pos):
    inv = 1.0 / (ROPE_BASE ** (jnp.arange(0, QK_ROPE_DIM, 2, dtype=F32) / QK_ROPE_DIM))
    ang = pos.astype(F32)[:, None] * inv[None, :]
    cos, sin = jnp.cos(ang), jnp.sin(ang)
    return jnp.concatenate([cos, cos], axis=-1), jnp.concatenate([-sin, sin], axis=-1)


def _half_swap_cols(w, n_groups):
    k = w.shape[0]
    half = QK_ROPE_DIM // 2
    return w.reshape(k, n_groups, 2, half)[:, :, ::-1, :].reshape(k, n_groups * QK_ROPE_DIM)


def kernel(x_prompt, x_sample, state_conv, cache_ckv, cache_kpe, page_table, g_mix, g_ffn, w_up, w_down, w_pw1, b_pw1, w_dw, b_dw, g_cln, b_cln, w_pw2, b_pw2, g_kv_in, w_dkv, g_ckv, w_kr, w_uk, w_uv, w_dq, g_q, w_uq, w_qr, w_o, g_final):
    d = D_MODEL
    row = lambda v: v.reshape(1, -1)

    wa, wb = w_pw1[0, :, :d].astype(BF16), w_pw1[0, :, d:].astype(BF16)
    ba, bb = row(b_pw1[0, :d]), row(b_pw1[0, d:])
    w2 = w_pw2[0].astype(BF16)
    wup, wdn = w_up.astype(BF16), w_down.astype(BF16)
    pad = jnp.zeros((d, LANES - QK_ROPE_DIM), F32)
    wkv = jnp.concatenate([w_dkv, w_kr, pad, _half_swap_cols(w_kr, 1), pad], axis=1).astype(BF16)
    wdq, wuq = w_dq[0].astype(BF16), w_uq[0].astype(BF16)
    wqr, wqs = w_qr[0].astype(BF16), _half_swap_cols(w_qr[0], N_HEADS).astype(BF16)
    wukt = jnp.transpose(w_uk, (1, 2, 0)).astype(BF16)
    wuv = jnp.transpose(w_uv, (1, 0, 2)).astype(BF16)
    wo = w_o[0].astype(BF16)

    def dense_front(x, hist, bn, tt, cos, sin):
        n, t, _ = x.shape
        xf = x.reshape(n * t, d)
        u = _pw1_glu(xf, row(g_mix[0]), wa, wb, ba, bb)
        t_pad = max(t, tt)
        u3, x3 = u.reshape(n, t, d), x
        if t_pad != t:
            zpad = ((0, 0), (0, t_pad - t), (0, 0))
            u3, x3 = jnp.pad(u3, zpad), jnp.pad(x3, zpad)
        x1 = _conv_tail(u3, hist, x3, w_dw[0], row(b_dw[0]), row(g_cln[0]), row(b_cln[0]),
                        w2, row(b_pw2[0]), bn, tt)[:, :t].reshape(n * t, d)
        x2 = _mlp(x1, row(g_ffn[0]), wup[0], wdn[0], row(g_final), False)
        ckv, kpe, kvb = _shared_kv(x2, row(g_kv_in), wkv, row(g_ckv), cos, sin)
        cos_h, sin_h = jnp.tile(cos, (1, N_HEADS)), jnp.tile(sin, (1, N_HEADS))
        q = _mla_q(x2, row(g_mix[1]), wdq, row(g_q[0]), wuq, wqr, wqs, wukt, cos_h, sin_h)
        return u, x2, ckv, kpe, kvb, q

    def dense_back(o_lat, x2):
        x3 = _attn_out(o_lat, x2, wuv, wo)
        return _mlp(x3, row(g_ffn[1]), wup[1], wdn[1], row(g_final), True)

    n_p, t_p, _ = x_prompt.shape
    cos_p, sin_p = _rope_tables(jnp.arange(t_p))
    hist_p = jnp.zeros((n_p, CONV_TAIL, d), F32)
    u_p, x2_p, ckv_p, kpe_p, kvb_p, q_p = dense_front(x_prompt, hist_p, 1, ROW_TILE, cos_p, sin_p)
    o_p = _prompt_attn(q_p, kvb_p, n_p, t_p)
    y_p = dense_back(o_p, x2_p).reshape(n_p, t_p, d)
    conv_p = u_p.reshape(n_p, t_p, d)[:, t_p - (CONV_WIDTH - 1):][None]

    n_s, t_s, _ = x_sample.shape
    past_len = page_table.shape[1] * cache_ckv.shape[1]
    cos_s, sin_s = _rope_tables(past_len + jnp.arange(t_s))
    cos_s, sin_s = jnp.tile(cos_s, (n_s, 1)), jnp.tile(sin_s, (n_s, 1))
    hist_s = jnp.pad(state_conv[0], ((0, 0), (CONV_TAIL - (CONV_WIDTH - 1), 0), (0, 0)))
    bn_s = ROW_TILE // SUBLANES
    u_s, x2_s, ckv_s, kpe_s, kvb_s, q_s = dense_front(x_sample, hist_s, bn_s, SUBLANES, cos_s, sin_s)
    rows_s = N_HEADS * t_s
    q_seq = q_s.reshape(N_HEADS, n_s, t_s, KV_DIM).transpose(1, 0, 2, 3).reshape(n_s, rows_s, KV_DIM)
    kv_new = jnp.pad(kvb_s.reshape(n_s, t_s, KV_DIM), ((0, 0), (0, NEW_ROWS_PAD - t_s), (0, 0)))
    o_seq = _sample_attn(page_table, q_seq, kv_new, cache_ckv, cache_kpe)
    o_s = o_seq.reshape(n_s, N_HEADS, t_s, KV_LORA).transpose(1, 0, 2, 3).reshape(N_HEADS, n_s * t_s, KV_LORA)
    y_s = dense_back(o_s, x2_s).reshape(n_s, t_s, d)
    conv_s = jnp.concatenate([state_conv[0][:, t_s:], u_s.reshape(n_s, t_s, d)], axis=1)[None]

    return (y_p, y_s, conv_p,
            ckv_p.reshape(n_p, t_p, KV_LORA), kpe_p.reshape(n_p, t_p, QK_ROPE_DIM),
            conv_s,
            ckv_s.reshape(n_s, t_s, KV_LORA), kpe_s.reshape(n_s, t_s, QK_ROPE_DIM))
```

```python
import functools

import jax
import jax.numpy as jnp
from jax import lax
from jax.experimental import pallas as pl
from jax.experimental.pallas import tpu as pltpu

D_MODEL = 1024
D_FF = 4 * D_MODEL
CONV_WIDTH = 31
N_HEADS = 8
QK_NOPE_DIM = 128
QK_ROPE_DIM = 64
V_HEAD_DIM = 128
Q_LORA = 768
KV_LORA = 256
KV_DIM = KV_LORA + QK_ROPE_DIM
ROPE_BASE = 10000.0
RMS_EPS = 1e-6
LN_EPS = 1e-5
ATTN_SCALE = (QK_NOPE_DIM + QK_ROPE_DIM) ** -0.5
EXP2_SCALE = ATTN_SCALE * 1.4426950408889634

LANES = 128
SUBLANES = 8
VMEM_LIMIT_BYTES = 56 * 1024 * 1024

ROW_TILE = 512
FF_TILE = 1024
CONV_TAIL = 32
CONV_ROW_CHUNK = 64
ATTN_TILE = 256

F32 = jnp.float32
BF16 = jnp.bfloat16


def _params(*semantics):
    return pltpu.CompilerParams(dimension_semantics=semantics,
                                vmem_limit_bytes=VMEM_LIMIT_BYTES)


def _rms(x, g):
    return x * lax.rsqrt(jnp.mean(x * x, axis=-1, keepdims=True) + RMS_EPS) * g


def _dot(a, b):
    return jnp.dot(a, b, preferred_element_type=F32)


def _dot_nt(a, b):
    return lax.dot_general(a, b, (((1,), (1,)), ((), ())), preferred_element_type=F32)


def _full(shape):
    return pl.BlockSpec(shape, lambda *_: (0,) * len(shape))


def _pw1_glu_kernel(x_ref, g_ref, wa_ref, wb_ref, ba_ref, bb_ref, u_ref):
    hn = _rms(x_ref[...], g_ref[...]).astype(BF16)
    a = _dot(hn, wa_ref[...]) + ba_ref[...]
    b = _dot(hn, wb_ref[...]) + bb_ref[...]
    u_ref[...] = a * jax.nn.sigmoid(b)


def _pw1_glu(x, g, wa, wb, ba, bb):
    m, d = x.shape
    tm = min(ROW_TILE, m)
    row = pl.BlockSpec((tm, d), lambda i: (i, 0))
    return pl.pallas_call(
        _pw1_glu_kernel,
        grid=(m // tm,),
        in_specs=[row, _full((1, d)), _full((d, d)), _full((d, d)), _full((1, d)), _full((1, d))],
        out_specs=row,
        out_shape=jax.ShapeDtypeStruct((m, d), F32),
        compiler_params=_params("arbitrary"),
        name="pw1_glu",
    )(x, g, wa, wb, ba, bb)


def _conv_kernel(u_ref, hist_ref, x_ref, wdw_ref, bdw_ref, gln_ref, bln_ref, w2_ref, b2_ref,
                 o_ref, win_ref, v_ref, slab_ref):
    bn, tt, d = u_ref.shape
    rc = min(CONV_ROW_CHUNK, tt)
    n_row_chunks = tt // rc

    @pl.when(pl.program_id(1) == 0)
    def _():
        win_ref[:, 0:CONV_TAIL, :] = hist_ref[...]

    win_ref[:, CONV_TAIL:CONV_TAIL + tt, :] = u_ref[...]

    lead = CONV_TAIL - (CONV_WIDTH - 1)

    def chunk(idx, carry):
        b = idx // n_row_chunks
        r0 = pl.multiple_of((idx % n_row_chunks) * rc, SUBLANES)
        slab_ref[...] = win_ref[b, pl.ds(r0, rc + CONV_TAIL), :]
        for c in range(d // LANES):
            cols = slice(c * LANES, (c + 1) * LANES)
            acc = jnp.zeros((rc, LANES), F32)
            for k in range(CONV_WIDTH):
                acc = acc + wdw_ref[k:k + 1, cols] * slab_ref[lead + k:lead + k + rc, cols]
            v_ref[b, pl.ds(r0, rc), cols] = acc
        return carry

    lax.fori_loop(0, bn * n_row_chunks, chunk, 0)

    win_ref[:, 0:CONV_TAIL, :] = win_ref[:, tt:tt + CONV_TAIL, :]

    v = v_ref[...].reshape(bn * tt, d) + bdw_ref[...]
    mu = jnp.mean(v, axis=-1, keepdims=True)
    vc = v - mu
    var = jnp.mean(vc * vc, axis=-1, keepdims=True)
    vn = vc * lax.rsqrt(var + LN_EPS) * gln_ref[...] + bln_ref[...]
    s = vn * jax.nn.sigmoid(vn)
    y = _dot(s.astype(BF16), w2_ref[...]) + b2_ref[...]
    o_ref[...] = x_ref[...] + y.reshape(bn, tt, d)


def _conv_tail(u, hist, x, wdw, bdw, gln, bln, w2, b2, bn, tt):
    n, t, d = u.shape
    blk = pl.BlockSpec((bn, tt, d), lambda i, j: (i, j, 0))
    return pl.pallas_call(
        _conv_kernel,
        grid=(n // bn, t // tt),
        in_specs=[blk,
                  pl.BlockSpec((bn, CONV_TAIL, d), lambda i, j: (i, 0, 0)),
                  blk,
                  _full((CONV_WIDTH, d)), _full((1, d)), _full((1, d)), _full((1, d)),
                  _full((d, d)), _full((1, d))],
        out_specs=blk,
        out_shape=jax.ShapeDtypeStruct((n, t, d), F32),
        scratch_shapes=[pltpu.VMEM((bn, CONV_TAIL + tt, d), F32),
                        pltpu.VMEM((bn, tt, d), F32),
                        pltpu.VMEM((min(CONV_ROW_CHUNK, tt) + CONV_TAIL, d), F32)],
        compiler_params=_params("arbitrary", "arbitrary"),
        name="conv_tail",
    )(u, hist, x, wdw, bdw, gln, bln, w2, b2)


def _mlp_kernel(final_norm, x_ref, g_ref, wup_ref, wdn_ref, gf_ref, o_ref, hn_ref, acc_ref):
    f = pl.program_id(1)

    @pl.when(f == 0)
    def _():
        x = x_ref[...]
        hn_ref[...] = _rms(x, g_ref[...]).astype(BF16)
        acc_ref[...] = x

    h = _dot(hn_ref[...], wup_ref[...])
    h = jnp.square(jnp.maximum(h, 0.0)).astype(BF16)
    acc_ref[...] += _dot(h, wdn_ref[...])

    @pl.when(f == pl.num_programs(1) - 1)
    def _():
        out = acc_ref[...]
        if final_norm:
            out = _rms(out, gf_ref[...])
        o_ref[...] = out


def _mlp(x, g, wup, wdn, gf, final_norm):
    m, d = x.shape
    tm = min(ROW_TILE, m)
    row = pl.BlockSpec((tm, d), lambda i, f: (i, 0))
    return pl.pallas_call(
        functools.partial(_mlp_kernel, final_norm),
        grid=(m // tm, D_FF // FF_TILE),
        in_specs=[row, _full((1, d)),
                  pl.BlockSpec((d, FF_TILE), lambda i, f: (0, f)),
                  pl.BlockSpec((FF_TILE, d), lambda i, f: (f, 0)),
                  _full((1, d))],
        out_specs=row,
        out_shape=jax.ShapeDtypeStruct((m, d), F32),
        scratch_shapes=[pltpu.VMEM((tm, d), BF16), pltpu.VMEM((tm, d), F32)],
        compiler_params=_params("arbitrary", "arbitrary"),
        name="mlp_final" if final_norm else "mlp",
    )(x, g, wup, wdn, gf)


KV_W_COLS = 4 * LANES


def _shared_kv_kernel(x_ref, g_ref, w_ref, gc_ref, cos_ref, sin_ref, ckv_ref, kpe_ref, kvb_ref):
    hn = _rms(x_ref[...], g_ref[...]).astype(BF16)
    a = _dot(hn, w_ref[...])
    ckv = _rms(a[:, :KV_LORA], gc_ref[...])
    kr = a[:, KV_LORA:KV_LORA + QK_ROPE_DIM]
    ks = a[:, KV_LORA + LANES:KV_LORA + LANES + QK_ROPE_DIM]
    kpe = kr * cos_ref[...] + ks * sin_ref[...]
    ckv_ref[...] = ckv
    kpe_ref[...] = kpe
    kvb_ref[:, :KV_LORA] = ckv.astype(BF16)
    kvb_ref[:, KV_LORA:] = kpe.astype(BF16)


def _shared_kv(x, g, w, gc, cos, sin):
    m, d = x.shape
    tm = min(ROW_TILE, m)
    n_pos = cos.shape[0] // tm
    row = lambda w_: pl.BlockSpec((tm, w_), lambda i: (i, 0))
    pos = pl.BlockSpec((tm, QK_ROPE_DIM), lambda i: (i % n_pos, 0))
    return pl.pallas_call(
        _shared_kv_kernel,
        grid=(m // tm,),
        in_specs=[row(d), _full((1, d)), _full((d, KV_W_COLS)), _full((1, KV_LORA)), pos, pos],
        out_specs=[row(KV_LORA), row(QK_ROPE_DIM), row(KV_DIM)],
        out_shape=[jax.ShapeDtypeStruct((m, KV_LORA), F32),
                   jax.ShapeDtypeStruct((m, QK_ROPE_DIM), F32),
                   jax.ShapeDtypeStruct((m, KV_DIM), BF16)],
        compiler_params=_params("arbitrary"),
        name="shared_kv",
    )(x, g, w, gc, cos, sin)


def _mla_q_kernel(x_ref, g_ref, wdq_ref, gq_ref, wuq_ref, wqr_ref, wqs_ref, wukt_ref,
                  cos_ref, sin_ref, q_ref):
    h = _rms(x_ref[...], g_ref[...]).astype(BF16)
    cq = _rms(_dot(h, wdq_ref[...]), gq_ref[...]).astype(BF16)
    qn = _dot(cq, wuq_ref[...]).astype(BF16)
    qpe = _dot(cq, wqr_ref[...]) * cos_ref[...] + _dot(cq, wqs_ref[...]) * sin_ref[...]
    for hd in range(N_HEADS):
        q_lat = _dot(qn[:, hd * QK_NOPE_DIM:(hd + 1) * QK_NOPE_DIM], wukt_ref[hd])
        q_ref[hd, :, :KV_LORA] = q_lat.astype(BF16)
        q_ref[hd, :, KV_LORA:] = qpe[:, hd * QK_ROPE_DIM:(hd + 1) * QK_ROPE_DIM].astype(BF16)


def _mla_q(x, g, wdq, gq, wuq, wqr, wqs, wukt, cos, sin):
    m, d = x.shape
    tm = min(ROW_TILE, m)
    n_pos = cos.shape[0] // tm
    rope_w = N_HEADS * QK_ROPE_DIM
    pos = pl.BlockSpec((tm, rope_w), lambda i: (i % n_pos, 0))
    return pl.pallas_call(
        _mla_q_kernel,
        grid=(m // tm,),
        in_specs=[pl.BlockSpec((tm, d), lambda i: (i, 0)), _full((1, d)),
                  _full((d, Q_LORA)), _full((1, Q_LORA)),
                  _full((Q_LORA, N_HEADS * QK_NOPE_DIM)),
                  _full((Q_LORA, rope_w)), _full((Q_LORA, rope_w)),
                  _full((N_HEADS, QK_NOPE_DIM, KV_LORA)), pos, pos],
        out_specs=pl.BlockSpec((N_HEADS, tm, KV_DIM), lambda i: (0, i, 0)),
        out_shape=jax.ShapeDtypeStruct((N_HEADS, m, KV_DIM), BF16),
        compiler_params=_params("arbitrary"),
        name="mla_q",
    )(x, g, wdq, gq, wuq, wqr, wqs, wukt, cos, sin)


def _lanes2(x):
    return jnp.concatenate([x] * (KV_LORA // LANES), axis=1)


def _prompt_attn_kernel(q_ref, kv_ref, o_ref, m_ref, l_ref, acc_ref):
    n_heads, tq, _ = q_ref.shape
    qi = pl.program_id(1)
    stat = (tq, LANES)

    q_all = q_ref[...].reshape(n_heads * tq, KV_DIM)

    def attend(kv, mask, first):
        s_all = _dot_nt(q_all, kv)
        ps, alphas = [], []
        for hd in range(n_heads):
            s = s_all[hd * tq:(hd + 1) * tq]
            if mask is not None:
                s = jnp.where(mask, s, -jnp.inf)
            m_cur = jnp.max(s, axis=-1, keepdims=True)
            if first:
                m_new = jnp.broadcast_to(m_cur, stat)
            else:
                m_prev = m_ref[hd]
                m_new = jnp.maximum(m_prev, m_cur)
            p = jnp.exp2((s - _lanes2(m_new)) * EXP2_SCALE)
            l_cur = jnp.sum(p, axis=-1, keepdims=True)
            ps.append(p.astype(BF16))
            if first:
                l_ref[hd] = jnp.broadcast_to(l_cur, stat)
            else:
                alpha = jnp.exp2((m_prev - m_new) * EXP2_SCALE)
                l_ref[hd] = alpha * l_ref[hd] + l_cur
                alphas.append(alpha)
            m_ref[hd] = m_new
        pv_all = _dot(jnp.concatenate(ps, axis=0), kv[:, :KV_LORA])
        for hd in range(n_heads):
            pv = pv_all[hd * tq:(hd + 1) * tq]
            if first:
                acc_ref[hd] = pv
            else:
                acc_ref[hd] = _lanes2(alphas[hd]) * acc_ref[hd] + pv

    k0 = pl.multiple_of(qi * tq, tq)
    t_idx = lax.broadcasted_iota(jnp.int32, (tq, tq), 0)
    c_idx = lax.broadcasted_iota(jnp.int32, (tq, tq), 1)
    attend(kv_ref[pl.ds(k0, tq), :], c_idx <= t_idx, True)

    def body(j, carry):
        kj = pl.multiple_of(j * tq, tq)
        attend(kv_ref[pl.ds(kj, tq), :], None, False)
        return carry

    lax.fori_loop(0, qi, body, 0)

    for hd in range(n_heads):
        o_ref[hd] = (acc_ref[hd] * _lanes2(1.0 / l_ref[hd])).astype(BF16)


def _prompt_attn(q, kvb, n, t):
    tq = ATTN_TILE
    nq = t // tq
    return pl.pallas_call(
        _prompt_attn_kernel,
        grid=(n, nq),
        in_specs=[pl.BlockSpec((N_HEADS, tq, KV_DIM), lambda b, i: (0, b * nq + i, 0)),
                  pl.BlockSpec((t, KV_DIM), lambda b, i: (b, 0))],
        out_specs=pl.BlockSpec((N_HEADS, tq, KV_LORA), lambda b, i: (0, b * nq + i, 0)),
        out_shape=jax.ShapeDtypeStruct((N_HEADS, n * t, KV_LORA), BF16),
        scratch_shapes=[pltpu.VMEM((N_HEADS, tq, LANES), F32),
                        pltpu.VMEM((N_HEADS, tq, LANES), F32),
                        pltpu.VMEM((N_HEADS, tq, KV_LORA), F32)],
        compiler_params=_params("arbitrary", "arbitrary"),
        name="prompt_attn",
    )(q, kvb)


NEW_ROWS_PAD = 16
CHUNK_PAGES = 32
RING_SLOTS = 3
DMA_UNROLL = 4


def _sample_attn_kernel(pt_ref, q_ref, new_ref, ckv_hbm, kpt_hbm, o_ref,
                        land_c, land_p, sem, kb_ref, kpt_ref, s_ref, p_ref):
    rows = q_ref.shape[1]
    n_new = rows // N_HEADS
    n_chunks = kb_ref.shape[0]
    page = land_c.shape[2]
    b = pl.program_id(0)
    total = pl.num_programs(0) * n_chunks
    ahead = RING_SLOTS - 1

    def page_copies(g, slot, i):
        pid = pt_ref[g * CHUNK_PAGES + i]
        return (pltpu.make_async_copy(ckv_hbm.at[pid], land_c.at[slot, i], sem.at[slot]),
                pltpu.make_async_copy(kpt_hbm.at[pid], land_p.at[slot, i], sem.at[slot]))

    def start_chunk(g):
        slot = g % RING_SLOTS

        def start_page(i, carry):
            for cp in page_copies(g, slot, i):
                cp.start()
            return carry

        lax.fori_loop(0, CHUNK_PAGES, start_page, 0, unroll=DMA_UNROLL)

    def wait_chunk(g):
        slot = g % RING_SLOTS

        def wait_page(i, carry):
            for cp in page_copies(g, slot, i):
                cp.wait()
            return carry

        lax.fori_loop(0, CHUNK_PAGES, wait_page, 0, unroll=DMA_UNROLL)

    @pl.when(b == 0)
    def _():
        for g0 in range(ahead):
            start_chunk(g0)

    q = q_ref[0]
    q_lat, q_pe = q[:, :KV_LORA], q[:, KV_LORA:]

    def chunk_body(c, carry):
        g = b * n_chunks + c

        @pl.when(g + ahead < total)
        def _():
            start_chunk(g + ahead)

        wait_chunk(g)
        slot = g % RING_SLOTS
        for i in range(CHUNK_PAGES):
            kb_ref[c, i * page:(i + 1) * page, :] = land_c[slot, i].astype(BF16)
            kpt_ref[c, :, i * page:(i + 1) * page] = land_p[slot, i].astype(BF16)
        s_ref[c] = _dot_nt(q_lat, kb_ref[c]) + _dot(q_pe, kpt_ref[c])
        return carry

    lax.fori_loop(0, n_chunks, chunk_body, 0)

    kn = new_ref[0]
    s_new = _dot_nt(q, kn)
    t_idx = lax.broadcasted_iota(jnp.int32, s_new.shape, 0) % n_new
    c_idx = lax.broadcasted_iota(jnp.int32, s_new.shape, 1)
    s_new = jnp.where(c_idx <= t_idx, s_new, -jnp.inf)
    s_all = s_ref[...]
    m = jnp.maximum(jnp.max(jnp.max(s_all, axis=0), axis=-1, keepdims=True),
                    jnp.max(s_new, axis=-1, keepdims=True))
    p_all = jnp.exp2((s_all - m) * EXP2_SCALE)
    p_new = jnp.exp2((s_new - m) * EXP2_SCALE)
    l = (jnp.sum(jnp.sum(p_all, axis=0), axis=-1, keepdims=True)
         + jnp.sum(p_new, axis=-1, keepdims=True))
    p_ref[...] = p_all.astype(BF16)
    o = _dot(p_new.astype(BF16), kn[:, :KV_LORA])
    for c in range(n_chunks):
        o = o + _dot(p_ref[c], kb_ref[c])
    o_ref[0] = (o * (1.0 / l)).astype(BF16)


def _sample_attn(page_table, q, kv_new, cache_ckv, cache_kpt):
    n_seq, n_pages = page_table.shape
    page = cache_ckv.shape[1]
    rows = q.shape[1]
    n_chunks = n_pages // CHUNK_PAGES
    width = CHUNK_PAGES * page
    per_seq = lambda r, w: pl.BlockSpec((1, r, w), lambda b, pt: (b, 0, 0))
    grid_spec = pltpu.PrefetchScalarGridSpec(
        num_scalar_prefetch=1,
        grid=(n_seq,),
        in_specs=[per_seq(rows, KV_DIM), per_seq(NEW_ROWS_PAD, KV_DIM),
                  pl.BlockSpec(memory_space=pl.ANY), pl.BlockSpec(memory_space=pl.ANY)],
        out_specs=per_seq(rows, KV_LORA),
        scratch_shapes=[pltpu.VMEM((RING_SLOTS, CHUNK_PAGES, page, KV_LORA), F32),
                        pltpu.VMEM((RING_SLOTS, CHUNK_PAGES, QK_ROPE_DIM, page), F32),
                        pltpu.SemaphoreType.DMA((RING_SLOTS,)),
                        pltpu.VMEM((n_chunks, width, KV_LORA), BF16),
                        pltpu.VMEM((n_chunks, QK_ROPE_DIM, width), BF16),
                        pltpu.VMEM((n_chunks, rows, width), F32),
                        pltpu.VMEM((n_chunks, rows, width), BF16)])
    return pl.pallas_call(
        _sample_attn_kernel,
        grid_spec=grid_spec,
        out_shape=jax.ShapeDtypeStruct((n_seq, rows, KV_LORA), BF16),
        compiler_params=_params("arbitrary"),
        name="sample_attn",
    )(page_table.reshape(-1), q, kv_new, cache_ckv, cache_kpt)


def _attn_out_kernel(o_ref, x_ref, wuv_ref, wo_ref, y_ref):
    parts = [_dot(o_ref[hd], wuv_ref[hd]) for hd in range(N_HEADS)]
    o = jnp.concatenate(parts, axis=-1).astype(BF16)
    y_ref[...] = x_ref[...] + _dot(o, wo_ref[...])


def _attn_out(o_lat, x, wuv, wo):
    m, d = x.shape
    tm = min(ROW_TILE, m)
    row = pl.BlockSpec((tm, d), lambda i: (i, 0))
    return pl.pallas_call(
        _attn_out_kernel,
        grid=(m // tm,),
        in_specs=[pl.BlockSpec((N_HEADS, tm, KV_LORA), lambda i: (0, i, 0)), row,
                  _full((N_HEADS, KV_LORA, V_HEAD_DIM)),
                  _full((N_HEADS * V_HEAD_DIM, d))],
        out_specs=row,
        out_shape=jax.ShapeDtypeStruct((m, d), F32),
        compiler_params=_params("arbitrary"),
        name="attn_out",
    )(o_lat, x, wuv, wo)


def _rope_tables(pos):
    inv = 1.0 / (ROPE_BASE ** (jnp.arange(0, QK_ROPE_DIM, 2, dtype=F32) / QK_ROPE_DIM))
    ang = pos.astype(F32)[:, None] * inv[None, :]
    cos, sin = jnp.cos(ang), jnp.sin(ang)
    return jnp.concatenate([cos, cos], axis=-1), jnp.concatenate([-sin, sin], axis=-1)


def _half_swap_cols(w, n_groups):
    k = w.shape[0]
    half = QK_ROPE_DIM // 2
    return w.reshape(k, n_groups, 2, half)[:, :, ::-1, :].reshape(k, n_groups * QK_ROPE_DIM)


def kernel(x_prompt, x_sample, state_conv, cache_ckv, cache_kpe, page_table, g_mix, g_ffn, w_up, w_down, w_pw1, b_pw1, w_dw, b_dw, g_cln, b_cln, w_pw2, b_pw2, g_kv_in, w_dkv, g_ckv, w_kr, w_uk, w_uv, w_dq, g_q, w_uq, w_qr, w_o, g_final):
    d = D_MODEL
    row = lambda v: v.reshape(1, -1)

    wa, wb = w_pw1[0, :, :d].astype(BF16), w_pw1[0, :, d:].astype(BF16)
    ba, bb = row(b_pw1[0, :d]), row(b_pw1[0, d:])
    w2 = w_pw2[0].astype(BF16)
    wup, wdn = w_up.astype(BF16), w_down.astype(BF16)
    pad = jnp.zeros((d, LANES - QK_ROPE_DIM), F32)
    wkv = jnp.concatenate([w_dkv, w_kr, pad, _half_swap_cols(w_kr, 1), pad], axis=1).astype(BF16)
    wdq, wuq = w_dq[0].astype(BF16), w_uq[0].astype(BF16)
    wqr, wqs = w_qr[0].astype(BF16), _half_swap_cols(w_qr[0], N_HEADS).astype(BF16)
    wukt = jnp.transpose(w_uk, (1, 2, 0)).astype(BF16)
    wuv = jnp.transpose(w_uv, (1, 0, 2)).astype(BF16)
    wo = w_o[0].astype(BF16)

    def dense_front(x, hist, bn, tt, cos, sin):
        n, t, _ = x.shape
        xf = x.reshape(n * t, d)
        u = _pw1_glu(xf, row(g_mix[0]), wa, wb, ba, bb)
        t_pad = max(t, tt)
        u3, x3 = u.reshape(n, t, d), x
        if t_pad != t:
            zpad = ((0, 0), (0, t_pad - t), (0, 0))
            u3, x3 = jnp.pad(u3, zpad), jnp.pad(x3, zpad)
        x1 = _conv_tail(u3, hist, x3, w_dw[0], row(b_dw[0]), row(g_cln[0]), row(b_cln[0]),
                        w2, row(b_pw2[0]), bn, tt)[:, :t].reshape(n * t, d)
        x2 = _mlp(x1, row(g_ffn[0]), wup[0], wdn[0], row(g_final), False)
        ckv, kpe, kvb = _shared_kv(x2, row(g_kv_in), wkv, row(g_ckv), cos, sin)
        cos_h, sin_h = jnp.tile(cos, (1, N_HEADS)), jnp.tile(sin, (1, N_HEADS))
        q = _mla_q(x2, row(g_mix[1]), wdq, row(g_q[0]), wuq, wqr, wqs, wukt, cos_h, sin_h)
        return u, x2, ckv, kpe, kvb, q

    def dense_back(o_lat, x2):
        x3 = _attn_out(o_lat, x2, wuv, wo)
        return _mlp(x3, row(g_ffn[1]), wup[1], wdn[1], row(g_final), True)

    n_p, t_p, _ = x_prompt.shape
    cos_p, sin_p = _rope_tables(jnp.arange(t_p))
    hist_p = jnp.zeros((n_p, CONV_TAIL, d), F32)
    u_p, x2_p, ckv_p, kpe_p, kvb_p, q_p = dense_front(x_prompt, hist_p, 1, ROW_TILE, cos_p, sin_p)
    o_p = _prompt_attn(q_p, kvb_p, n_p, t_p)
    y_p = dense_back(o_p, x2_p).reshape(n_p, t_p, d)
    conv_p = u_p.reshape(n_p, t_p, d)[:, t_p - (CONV_WIDTH - 1):][None]

    n_s, t_s, _ = x_sample.shape
    past_len = page_table.shape[1] * cache_ckv.shape[1]
    cos_s, sin_s = _rope_tables(past_len + jnp.arange(t_s))
    cos_s, sin_s = jnp.tile(cos_s, (n_s, 1)), jnp.tile(sin_s, (n_s, 1))
    hist_s = jnp.pad(state_conv[0], ((0, 0), (CONV_TAIL - (CONV_WIDTH - 1), 0), (0, 0)))
    bn_s = ROW_TILE // SUBLANES
    u_s, x2_s, ckv_s, kpe_s, kvb_s, q_s = dense_front(x_sample, hist_s, bn_s, SUBLANES, cos_s, sin_s)
    rows_s = N_HEADS * t_s
    q_seq = q_s.reshape(N_HEADS, n_s, t_s, KV_DIM).transpose(1, 0, 2, 3).reshape(n_s, rows_s, KV_DIM)
    kv_new = jnp.pad(kvb_s.reshape(n_s, t_s, KV_DIM), ((0, 0), (0, NEW_ROWS_PAD - t_s), (0, 0)))
    o_seq = _sample_attn(page_table, q_seq, kv_new, cache_ckv, jnp.swapaxes(cache_kpe, 1, 2))
    o_s = o_seq.reshape(n_s, N_HEADS, t_s, KV_LORA).transpose(1, 0, 2, 3).reshape(N_HEADS, n_s * t_s, KV_LORA)
    y_s = dense_back(o_s, x2_s).reshape(n_s, t_s, d)
    conv_s = jnp.concatenate([state_conv[0][:, t_s:], u_s.reshape(n_s, t_s, d)], axis=1)[None]

    return (y_p, y_s, conv_p,
            ckv_p.reshape(n_p, t_p, KV_LORA), kpe_p.reshape(n_p, t_p, QK_ROPE_DIM),
            conv_s,
            ckv_s.reshape(n_s, t_s, KV_LORA), kpe_s.reshape(n_s, t_s, QK_ROPE_DIM))
```

```python
import functools

import jax
import jax.numpy as jnp
from jax import lax
from jax.experimental import pallas as pl
from jax.experimental.pallas import tpu as pltpu

D_MODEL = 1024
D_FF = 4 * D_MODEL
CONV_WIDTH = 31
N_HEADS = 8
QK_NOPE_DIM = 128
QK_ROPE_DIM = 64
V_HEAD_DIM = 128
Q_LORA = 768
KV_LORA = 256
KV_DIM = KV_LORA + QK_ROPE_DIM
ROPE_BASE = 10000.0
RMS_EPS = 1e-6
LN_EPS = 1e-5
ATTN_SCALE = (QK_NOPE_DIM + QK_ROPE_DIM) ** -0.5
EXP2_SCALE = ATTN_SCALE * 1.4426950408889634

LANES = 128
SUBLANES = 8
VMEM_LIMIT_BYTES = 56 * 1024 * 1024

ROW_TILE = 512
MLP_ROW_TILE = 1024
FF_TILE = 1024
CONV_TAIL = 32
CONV_ROW_CHUNK = 64
ATTN_TILE = 256

F32 = jnp.float32
BF16 = jnp.bfloat16


def _params(*semantics, flags=None):
    return pltpu.CompilerParams(dimension_semantics=semantics,
                                vmem_limit_bytes=VMEM_LIMIT_BYTES, flags=flags)


def _rms(x, g):
    return x * lax.rsqrt(jnp.mean(x * x, axis=-1, keepdims=True) + RMS_EPS) * g


def _dot(a, b):
    return jnp.dot(a, b, preferred_element_type=F32)


def _dot_nt(a, b):
    return lax.dot_general(a, b, (((1,), (1,)), ((), ())), preferred_element_type=F32)


def _full(shape):
    return pl.BlockSpec(shape, lambda *_: (0,) * len(shape))


def _pw1_glu_kernel(x_ref, g_ref, wa_ref, wb_ref, ba_ref, bb_ref, u_ref):
    hn = _rms(x_ref[...], g_ref[...]).astype(BF16)
    a = _dot(hn, wa_ref[...]) + ba_ref[...]
    b = _dot(hn, wb_ref[...]) + bb_ref[...]
    u_ref[...] = a * jax.nn.sigmoid(b)


def _pw1_glu(x, g, wa, wb, ba, bb):
    m, d = x.shape
    tm = min(ROW_TILE, m)
    row = pl.BlockSpec((tm, d), lambda i: (i, 0))
    return pl.pallas_call(
        _pw1_glu_kernel,
        grid=(m // tm,),
        in_specs=[row, _full((1, d)), _full((d, d)), _full((d, d)), _full((1, d)), _full((1, d))],
        out_specs=row,
        out_shape=jax.ShapeDtypeStruct((m, d), F32),
        compiler_params=_params("arbitrary"),
        name="pw1_glu",
    )(x, g, wa, wb, ba, bb)


def _conv_kernel(u_ref, hist_ref, x_ref, wdw_ref, bdw_ref, gln_ref, bln_ref, w2_ref, b2_ref,
                 o_ref, win_ref, v_ref, slab_ref, shift_ref):
    bn, tt, d = u_ref.shape
    rc = min(CONV_ROW_CHUNK, tt)
    n_row_chunks = tt // rc

    @pl.when(pl.program_id(1) == 0)
    def _():
        win_ref[:, 0:CONV_TAIL, :] = hist_ref[...]

    win_ref[:, CONV_TAIL:CONV_TAIL + tt, :] = u_ref[...]

    lead = CONV_TAIL - (CONV_WIDTH - 1)

    def chunk(idx, carry):
        b = idx // n_row_chunks
        r0 = pl.multiple_of((idx % n_row_chunks) * rc, SUBLANES)
        slab_ref[...] = win_ref[b, pl.ds(r0, rc + CONV_TAIL), :]
        for res in range(1, SUBLANES):
            shift_ref[res] = slab_ref[res:res + shift_ref.shape[1], :]
        for c in range(d // LANES):
            cols = slice(c * LANES, (c + 1) * LANES)
            acc = jnp.zeros((rc, LANES), F32)
            for k in range(CONV_WIDTH):
                res = (lead + k) % SUBLANES
                a = (lead + k) - res
                if res == 0:
                    xk = slab_ref[a:a + rc, cols]
                else:
                    xk = shift_ref[res, a:a + rc, cols]
                acc = acc + wdw_ref[k:k + 1, cols] * xk
            v_ref[b, pl.ds(r0, rc), cols] = acc
        return carry

    lax.fori_loop(0, bn * n_row_chunks, chunk, 0)

    win_ref[:, 0:CONV_TAIL, :] = win_ref[:, tt:tt + CONV_TAIL, :]

    v = v_ref[...].reshape(bn * tt, d) + bdw_ref[...]
    mu = jnp.mean(v, axis=-1, keepdims=True)
    vc = v - mu
    var = jnp.mean(vc * vc, axis=-1, keepdims=True)
    vn = vc * lax.rsqrt(var + LN_EPS) * gln_ref[...] + bln_ref[...]
    s = vn * jax.nn.sigmoid(vn)
    y = _dot(s.astype(BF16), w2_ref[...]) + b2_ref[...]
    o_ref[...] = x_ref[...] + y.reshape(bn, tt, d)


def _conv_tail(u, hist, x, wdw, bdw, gln, bln, w2, b2, bn, tt):
    n, t, d = u.shape
    blk = pl.BlockSpec((bn, tt, d), lambda i, j: (i, j, 0))
    return pl.pallas_call(
        _conv_kernel,
        grid=(n // bn, t // tt),
        in_specs=[blk,
                  pl.BlockSpec((bn, CONV_TAIL, d), lambda i, j: (i, 0, 0)),
                  blk,
                  _full((CONV_WIDTH, d)), _full((1, d)), _full((1, d)), _full((1, d)),
                  _full((d, d)), _full((1, d))],
        out_specs=blk,
        out_shape=jax.ShapeDtypeStruct((n, t, d), F32),
        scratch_shapes=[pltpu.VMEM((bn, CONV_TAIL + tt, d), F32),
                        pltpu.VMEM((bn, tt, d), F32),
                        pltpu.VMEM((min(CONV_ROW_CHUNK, tt) + CONV_TAIL, d), F32),
                        pltpu.VMEM((SUBLANES, min(CONV_ROW_CHUNK, tt) + CONV_TAIL - SUBLANES, d), F32)],
        compiler_params=_params("arbitrary", "arbitrary"),
        name="conv_tail",
    )(u, hist, x, wdw, bdw, gln, bln, w2, b2)


def _mlp_kernel(final_norm, x_ref, g_ref, wup_ref, wdn_ref, gf_ref, o_ref, hn_ref, acc_ref):
    f = pl.program_id(1)

    @pl.when(f == 0)
    def _():
        x = x_ref[...]
        hn_ref[...] = _rms(x, g_ref[...]).astype(BF16)
        acc_ref[...] = x

    h = _dot(hn_ref[...], wup_ref[...])
    h = jnp.square(jnp.maximum(h, 0.0)).astype(BF16)
    acc_ref[...] += _dot(h, wdn_ref[...])

    @pl.when(f == pl.num_programs(1) - 1)
    def _():
        out = acc_ref[...]
        if final_norm:
            out = _rms(out, gf_ref[...])
        o_ref[...] = out


def _mlp(x, g, wup, wdn, gf, final_norm):
    m, d = x.shape
    tm = min(MLP_ROW_TILE, m)
    row = pl.BlockSpec((tm, d), lambda i, f: (i, 0))
    return pl.pallas_call(
        functools.partial(_mlp_kernel, final_norm),
        grid=(m // tm, D_FF // FF_TILE),
        in_specs=[row, _full((1, d)),
                  pl.BlockSpec((d, FF_TILE), lambda i, f: (0, f)),
                  pl.BlockSpec((FF_TILE, d), lambda i, f: (f, 0)),
                  _full((1, d))],
        out_specs=row,
        out_shape=jax.ShapeDtypeStruct((m, d), F32),
        scratch_shapes=[pltpu.VMEM((tm, d), BF16), pltpu.VMEM((tm, d), F32)],
        compiler_params=_params("arbitrary", "arbitrary"),
        name="mlp_final" if final_norm else "mlp",
    )(x, g, wup, wdn, gf)


KV_W_COLS = 4 * LANES


def _shared_kv_kernel(x_ref, g_ref, w_ref, gc_ref, cos_ref, sin_ref, ckv_ref, kpe_ref, kvb_ref):
    hn = _rms(x_ref[...], g_ref[...]).astype(BF16)
    a = _dot(hn, w_ref[...])
    ckv = _rms(a[:, :KV_LORA], gc_ref[...])
    kr = a[:, KV_LORA:KV_LORA + QK_ROPE_DIM]
    ks = a[:, KV_LORA + LANES:KV_LORA + LANES + QK_ROPE_DIM]
    kpe = kr * cos_ref[...] + ks * sin_ref[...]
    ckv_ref[...] = ckv
    kpe_ref[...] = kpe
    kvb_ref[:, :KV_LORA] = ckv.astype(BF16)
    kvb_ref[:, KV_LORA:] = kpe.astype(BF16)


def _shared_kv(x, g, w, gc, cos, sin):
    m, d = x.shape
    tm = min(ROW_TILE, m)
    n_pos = cos.shape[0] // tm
    row = lambda w_: pl.BlockSpec((tm, w_), lambda i: (i, 0))
    pos = pl.BlockSpec((tm, QK_ROPE_DIM), lambda i: (i % n_pos, 0))
    return pl.pallas_call(
        _shared_kv_kernel,
        grid=(m // tm,),
        in_specs=[row(d), _full((1, d)), _full((d, KV_W_COLS)), _full((1, KV_LORA)), pos, pos],
        out_specs=[row(KV_LORA), row(QK_ROPE_DIM), row(KV_DIM)],
        out_shape=[jax.ShapeDtypeStruct((m, KV_LORA), F32),
                   jax.ShapeDtypeStruct((m, QK_ROPE_DIM), F32),
                   jax.ShapeDtypeStruct((m, KV_DIM), BF16)],
        compiler_params=_params("arbitrary"),
        name="shared_kv",
    )(x, g, w, gc, cos, sin)


def _mla_q_kernel(x_ref, g_ref, wdq_ref, gq_ref, wuq_ref, wqr_ref, wqs_ref, wukt_ref,
                  cos_ref, sin_ref, q_ref):
    h = _rms(x_ref[...], g_ref[...]).astype(BF16)
    cq = _rms(_dot(h, wdq_ref[...]), gq_ref[...]).astype(BF16)
    qn = _dot(cq, wuq_ref[...]).astype(BF16)
    qpe = _dot(cq, wqr_ref[...]) * cos_ref[...] + _dot(cq, wqs_ref[...]) * sin_ref[...]
    for hd in range(N_HEADS):
        q_lat = _dot(qn[:, hd * QK_NOPE_DIM:(hd + 1) * QK_NOPE_DIM], wukt_ref[hd])
        q_ref[hd, :, :KV_LORA] = q_lat.astype(BF16)
        q_ref[hd, :, KV_LORA:] = qpe[:, hd * QK_ROPE_DIM:(hd + 1) * QK_ROPE_DIM].astype(BF16)


def _mla_q(x, g, wdq, gq, wuq, wqr, wqs, wukt, cos, sin):
    m, d = x.shape
    tm = min(ROW_TILE, m)
    n_pos = cos.shape[0] // tm
    rope_w = N_HEADS * QK_ROPE_DIM
    pos = pl.BlockSpec((tm, rope_w), lambda i: (i % n_pos, 0))
    return pl.pallas_call(
        _mla_q_kernel,
        grid=(m // tm,),
        in_specs=[pl.BlockSpec((tm, d), lambda i: (i, 0)), _full((1, d)),
                  _full((d, Q_LORA)), _full((1, Q_LORA)),
                  _full((Q_LORA, N_HEADS * QK_NOPE_DIM)),
                  _full((Q_LORA, rope_w)), _full((Q_LORA, rope_w)),
                  _full((N_HEADS, QK_NOPE_DIM, KV_LORA)), pos, pos],
        out_specs=pl.BlockSpec((N_HEADS, tm, KV_DIM), lambda i: (0, i, 0)),
        out_shape=jax.ShapeDtypeStruct((N_HEADS, m, KV_DIM), BF16),
        compiler_params=_params("arbitrary"),
        name="mla_q",
    )(x, g, wdq, gq, wuq, wqr, wqs, wukt, cos, sin)


def _lanes2(x):
    return jnp.concatenate([x] * (KV_LORA // LANES), axis=1)


def _prompt_attn_kernel(q_ref, kv_ref, o_ref, m_ref, l_ref, acc_ref):
    n_heads, tq, _ = q_ref.shape
    qi = pl.program_id(1)
    stat = (tq, LANES)

    q_all = q_ref[...].reshape(n_heads * tq, KV_DIM)

    def attend(kv, mask, first):
        s_all = _dot_nt(q_all, kv)
        ps, alphas = [], []
        for hd in range(n_heads):
            s = s_all[hd * tq:(hd + 1) * tq]
            if mask is not None:
                s = jnp.where(mask, s, -jnp.inf)
            m_cur = jnp.max(s, axis=-1, keepdims=True)
            if first:
                m_new = jnp.broadcast_to(m_cur, stat)
            else:
                m_prev = m_ref[hd]
                m_new = jnp.maximum(m_prev, m_cur)
            p = jnp.exp2((s - _lanes2(m_new)) * EXP2_SCALE)
            l_cur = jnp.sum(p, axis=-1, keepdims=True)
            ps.append(p.astype(BF16))
            if first:
                l_ref[hd] = jnp.broadcast_to(l_cur, stat)
            else:
                alpha = jnp.exp2((m_prev - m_new) * EXP2_SCALE)
                l_ref[hd] = alpha * l_ref[hd] + l_cur
                alphas.append(alpha)
            m_ref[hd] = m_new
        pv_all = _dot(jnp.concatenate(ps, axis=0), kv[:, :KV_LORA])
        for hd in range(n_heads):
            pv = pv_all[hd * tq:(hd + 1) * tq]
            if first:
                acc_ref[hd] = pv
            else:
                acc_ref[hd] = _lanes2(alphas[hd]) * acc_ref[hd] + pv

    k0 = pl.multiple_of(qi * tq, tq)
    t_idx = lax.broadcasted_iota(jnp.int32, (tq, tq), 0)
    c_idx = lax.broadcasted_iota(jnp.int32, (tq, tq), 1)
    attend(kv_ref[pl.ds(k0, tq), :], c_idx <= t_idx, True)

    def body(j, carry):
        kj = pl.multiple_of(j * tq, tq)
        attend(kv_ref[pl.ds(kj, tq), :], None, False)
        return carry

    lax.fori_loop(0, qi, body, 0)

    for hd in range(n_heads):
        o_ref[hd] = (acc_ref[hd] * _lanes2(1.0 / l_ref[hd])).astype(BF16)


def _prompt_attn(q, kvb, n, t):
    tq = ATTN_TILE
    nq = t // tq
    return pl.pallas_call(
        _prompt_attn_kernel,
        grid=(n, nq),
        in_specs=[pl.BlockSpec((N_HEADS, tq, KV_DIM), lambda b, i: (0, b * nq + i, 0)),
                  pl.BlockSpec((t, KV_DIM), lambda b, i: (b, 0))],
        out_specs=pl.BlockSpec((N_HEADS, tq, KV_LORA), lambda b, i: (0, b * nq + i, 0)),
        out_shape=jax.ShapeDtypeStruct((N_HEADS, n * t, KV_LORA), BF16),
        scratch_shapes=[pltpu.VMEM((N_HEADS, tq, LANES), F32),
                        pltpu.VMEM((N_HEADS, tq, LANES), F32),
                        pltpu.VMEM((N_HEADS, tq, KV_LORA), F32)],
        compiler_params=_params("arbitrary", "arbitrary"),
        name="prompt_attn",
    )(q, kvb)


NEW_ROWS_PAD = 16
CHUNK_PAGES = 32
RING_SLOTS = 4
DMA_UNROLL = 4


def _sample_attn_kernel(pt_ref, q_ref, new_ref, ckv_hbm, kpt_hbm, o_ref,
                        land_c, land_p, sem, kb_ref, kpt_ref, s_ref, p_ref):
    rows = q_ref.shape[1]
    n_new = rows // N_HEADS
    n_chunks = kb_ref.shape[0]
    page = land_c.shape[2]
    b = pl.program_id(0)
    total = pl.num_programs(0) * n_chunks
    ahead = RING_SLOTS - 1

    def page_copies(g, slot, i):
        pid = pt_ref[g * CHUNK_PAGES + i]
        return (pltpu.make_async_copy(ckv_hbm.at[pid], land_c.at[slot, i], sem.at[slot]),
                pltpu.make_async_copy(kpt_hbm.at[pid], land_p.at[slot, i], sem.at[slot]))

    def start_chunk(g):
        slot = g % RING_SLOTS

        def start_pair(j, carry):
            for parity in range(2):
                c_copy, p_copy = page_copies(g, slot, 2 * j + parity)
                c_copy.start(priority=parity)
                p_copy.start(priority=1 - parity)
            return carry

        lax.fori_loop(0, CHUNK_PAGES // 2, start_pair, 0, unroll=DMA_UNROLL // 2)

    def wait_chunk(g):
        slot = g % RING_SLOTS

        def wait_page(i, carry):
            for cp in page_copies(g, slot, i):
                cp.wait()
            return carry

        lax.fori_loop(0, CHUNK_PAGES, wait_page, 0, unroll=DMA_UNROLL)

    @pl.when(b == 0)
    def _():
        for g0 in range(ahead):
            start_chunk(g0)

    q = q_ref[0]
    q_lat, q_pe = q[:, :KV_LORA], q[:, KV_LORA:]

    def chunk_body(c, carry):
        g = b * n_chunks + c

        @pl.when(g + ahead < total)
        def _():
            start_chunk(g + ahead)

        wait_chunk(g)
        slot = g % RING_SLOTS
        for i in range(CHUNK_PAGES):
            kb_ref[c, i * page:(i + 1) * page, :] = land_c[slot, i].astype(BF16)
            kpt_ref[c, :, i * page:(i + 1) * page] = land_p[slot, i].astype(BF16)
        s_ref[c] = _dot_nt(q_lat, kb_ref[c]) + _dot(q_pe, kpt_ref[c])
        return carry

    lax.fori_loop(0, n_chunks, chunk_body, 0)

    kn = new_ref[0]
    s_new = _dot_nt(q, kn)
    t_idx = lax.broadcasted_iota(jnp.int32, s_new.shape, 0) % n_new
    c_idx = lax.broadcasted_iota(jnp.int32, s_new.shape, 1)
    s_new = jnp.where(c_idx <= t_idx, s_new, -jnp.inf)
    s_all = s_ref[...]
    m = jnp.maximum(jnp.max(jnp.max(s_all, axis=0), axis=-1, keepdims=True),
                    jnp.max(s_new, axis=-1, keepdims=True))
    p_all = jnp.exp2((s_all - m) * EXP2_SCALE)
    p_new = jnp.exp2((s_new - m) * EXP2_SCALE)
    l = (jnp.sum(jnp.sum(p_all, axis=0), axis=-1, keepdims=True)
         + jnp.sum(p_new, axis=-1, keepdims=True))
    p_ref[...] = p_all.astype(BF16)
    o = _dot(p_new.astype(BF16), kn[:, :KV_LORA])
    for c in range(n_chunks):
        o = o + _dot(p_ref[c], kb_ref[c])
    o_ref[0] = (o * (1.0 / l)).astype(BF16)


def _sample_attn(page_table, q, kv_new, cache_ckv, cache_kpt):
    n_seq, n_pages = page_table.shape
    page = cache_ckv.shape[1]
    rows = q.shape[1]
    n_chunks = n_pages // CHUNK_PAGES
    width = CHUNK_PAGES * page
    per_seq = lambda r, w: pl.BlockSpec((1, r, w), lambda b, pt: (b, 0, 0))
    grid_spec = pltpu.PrefetchScalarGridSpec(
        num_scalar_prefetch=1,
        grid=(n_seq,),
        in_specs=[per_seq(rows, KV_DIM), per_seq(NEW_ROWS_PAD, KV_DIM),
                  pl.BlockSpec(memory_space=pl.ANY), pl.BlockSpec(memory_space=pl.ANY)],
        out_specs=per_seq(rows, KV_LORA),
        scratch_shapes=[pltpu.VMEM((RING_SLOTS, CHUNK_PAGES, page, KV_LORA), F32),
                        pltpu.VMEM((RING_SLOTS, CHUNK_PAGES, QK_ROPE_DIM, page), F32),
                        pltpu.SemaphoreType.DMA((RING_SLOTS,)),
                        pltpu.VMEM((n_chunks, width, KV_LORA), BF16),
                        pltpu.VMEM((n_chunks, QK_ROPE_DIM, width), BF16),
                        pltpu.VMEM((n_chunks, rows, width), F32),
                        pltpu.VMEM((n_chunks, rows, width), BF16)])
    return pl.pallas_call(
        _sample_attn_kernel,
        grid_spec=grid_spec,
        out_shape=jax.ShapeDtypeStruct((n_seq, rows, KV_LORA), BF16),
        compiler_params=_params("arbitrary"),
        name="sample_attn",
    )(page_table.reshape(-1), q, kv_new, cache_ckv, cache_kpt)


def _attn_out_kernel(o_ref, x_ref, wuv_ref, wo_ref, y_ref):
    parts = [_dot(o_ref[hd], wuv_ref[hd]) for hd in range(N_HEADS)]
    o = jnp.concatenate(parts, axis=-1).astype(BF16)
    y_ref[...] = x_ref[...] + _dot(o, wo_ref[...])


def _attn_out(o_lat, x, wuv, wo):
    m, d = x.shape
    tm = min(ROW_TILE, m)
    row = pl.BlockSpec((tm, d), lambda i: (i, 0))
    return pl.pallas_call(
        _attn_out_kernel,
        grid=(m // tm,),
        in_specs=[pl.BlockSpec((N_HEADS, tm, KV_LORA), lambda i: (0, i, 0)), row,
                  _full((N_HEADS, KV_LORA, V_HEAD_DIM)),
                  _full((N_HEADS * V_HEAD_DIM, d))],
        out_specs=row,
        out_shape=jax.ShapeDtypeStruct((m, d), F32),
        compiler_params=_params("arbitrary"),
        name="attn_out",
    )(o_lat, x, wuv, wo)


def _rope_tables(pos):
    inv = 1.0 / (ROPE_BASE ** (jnp.arange(0, QK_ROPE_DIM, 2, dtype=F32) / QK_ROPE_DIM))
    ang = pos.astype(F32)[:, None] * inv[None, :]
    cos, sin = jnp.cos(ang), jnp.sin(ang)
    return jnp.concatenate([cos, cos], axis=-1), jnp.concatenate([-sin, sin], axis=-1)


def _half_swap_cols(w, n_groups):
    k = w.shape[0]
    half = QK_ROPE_DIM // 2
    return w.reshape(k, n_groups, 2, half)[:, :, ::-1, :].reshape(k, n_groups * QK_ROPE_DIM)


def kernel(x_prompt, x_sample, state_conv, cache_ckv, cache_kpe, page_table, g_mix, g_ffn, w_up, w_down, w_pw1, b_pw1, w_dw, b_dw, g_cln, b_cln, w_pw2, b_pw2, g_kv_in, w_dkv, g_ckv, w_kr, w_uk, w_uv, w_dq, g_q, w_uq, w_qr, w_o, g_final):
    d = D_MODEL
    row = lambda v: v.reshape(1, -1)

    wa, wb = w_pw1[0, :, :d].astype(BF16), w_pw1[0, :, d:].astype(BF16)
    ba, bb = row(b_pw1[0, :d]), row(b_pw1[0, d:])
    w2 = w_pw2[0].astype(BF16)
    wup, wdn = w_up.astype(BF16), w_down.astype(BF16)
    pad = jnp.zeros((d, LANES - QK_ROPE_DIM), F32)
    wkv = jnp.concatenate([w_dkv, w_kr, pad, _half_swap_cols(w_kr, 1), pad], axis=1).astype(BF16)
    wdq, wuq = w_dq[0].astype(BF16), w_uq[0].astype(BF16)
    wqr, wqs = w_qr[0].astype(BF16), _half_swap_cols(w_qr[0], N_HEADS).astype(BF16)
    wukt = jnp.transpose(w_uk, (1, 2, 0)).astype(BF16)
    wuv = jnp.transpose(w_uv, (1, 0, 2)).astype(BF16)
    wo = w_o[0].astype(BF16)

    def dense_front(x, hist, bn, tt, cos, sin):
        n, t, _ = x.shape
        xf = x.reshape(n * t, d)
        u = _pw1_glu(xf, row(g_mix[0]), wa, wb, ba, bb)
        t_pad = max(t, tt)
        u3, x3 = u.reshape(n, t, d), x
        if t_pad != t:
            zpad = ((0, 0), (0, t_pad - t), (0, 0))
            u3, x3 = jnp.pad(u3, zpad), jnp.pad(x3, zpad)
        x1 = _conv_tail(u3, hist, x3, w_dw[0], row(b_dw[0]), row(g_cln[0]), row(b_cln[0]),
                        w2, row(b_pw2[0]), bn, tt)[:, :t].reshape(n * t, d)
        x2 = _mlp(x1, row(g_ffn[0]), wup[0], wdn[0], row(g_final), False)
        ckv, kpe, kvb = _shared_kv(x2, row(g_kv_in), wkv, row(g_ckv), cos, sin)
        cos_h, sin_h = jnp.tile(cos, (1, N_HEADS)), jnp.tile(sin, (1, N_HEADS))
        q = _mla_q(x2, row(g_mix[1]), wdq, row(g_q[0]), wuq, wqr, wqs, wukt, cos_h, sin_h)
        return u, x2, ckv, kpe, kvb, q

    def dense_back(o_lat, x2):
        x3 = _attn_out(o_lat, x2, wuv, wo)
        return _mlp(x3, row(g_ffn[1]), wup[1], wdn[1], row(g_final), True)

    n_p, t_p, _ = x_prompt.shape
    cos_p, sin_p = _rope_tables(jnp.arange(t_p))
    hist_p = jnp.zeros((n_p, CONV_TAIL, d), F32)
    u_p, x2_p, ckv_p, kpe_p, kvb_p, q_p = dense_front(x_prompt, hist_p, 1, ROW_TILE, cos_p, sin_p)
    o_p = _prompt_attn(q_p, kvb_p, n_p, t_p)
    y_p = dense_back(o_p, x2_p).reshape(n_p, t_p, d)
    conv_p = u_p.reshape(n_p, t_p, d)[:, t_p - (CONV_WIDTH - 1):][None]

    n_s, t_s, _ = x_sample.shape
    past_len = page_table.shape[1] * cache_ckv.shape[1]
    cos_s, sin_s = _rope_tables(past_len + jnp.arange(t_s))
    cos_s, sin_s = jnp.tile(cos_s, (n_s, 1)), jnp.tile(sin_s, (n_s, 1))
    hist_s = jnp.pad(state_conv[0], ((0, 0), (CONV_TAIL - (CONV_WIDTH - 1), 0), (0, 0)))
    bn_s = ROW_TILE // SUBLANES
    u_s, x2_s, ckv_s, kpe_s, kvb_s, q_s = dense_front(x_sample, hist_s, bn_s, SUBLANES, cos_s, sin_s)
    rows_s = N_HEADS * t_s
    q_seq = q_s.reshape(N_HEADS, n_s, t_s, KV_DIM).transpose(1, 0, 2, 3).reshape(n_s, rows_s, KV_DIM)
    kv_new = jnp.pad(kvb_s.reshape(n_s, t_s, KV_DIM), ((0, 0), (0, NEW_ROWS_PAD - t_s), (0, 0)))
    o_seq = _sample_attn(page_table, q_seq, kv_new, cache_ckv, jnp.swapaxes(cache_kpe, 1, 2))
    o_s = o_seq.reshape(n_s, N_HEADS, t_s, KV_LORA).transpose(1, 0, 2, 3).reshape(N_HEADS, n_s * t_s, KV_LORA)
    y_s = dense_back(o_s, x2_s).reshape(n_s, t_s, d)
    conv_s = jnp.concatenate([state_conv[0][:, t_s:], u_s.reshape(n_s, t_s, d)], axis=1)[None]

    return (y_p, y_s, conv_p,
            ckv_p.reshape(n_p, t_p, KV_LORA), kpe_p.reshape(n_p, t_p, QK_ROPE_DIM),
            conv_s,
            ckv_s.reshape(n_s, t_s, KV_LORA), kpe_s.reshape(n_s, t_s, QK_ROPE_DIM))
```

```python
import functools

import jax
import jax.numpy as jnp
from jax import lax
from jax.experimental import pallas as pl
from jax.experimental.pallas import tpu as pltpu

D_MODEL = 1024
D_FF = 4 * D_MODEL
CONV_WIDTH = 31
N_HEADS = 8
QK_NOPE_DIM = 128
QK_ROPE_DIM = 64
V_HEAD_DIM = 128
Q_LORA = 768
KV_LORA = 256
KV_DIM = KV_LORA + QK_ROPE_DIM
ROPE_BASE = 10000.0
RMS_EPS = 1e-6
LN_EPS = 1e-5
ATTN_SCALE = (QK_NOPE_DIM + QK_ROPE_DIM) ** -0.5
EXP2_SCALE = ATTN_SCALE * 1.4426950408889634

LANES = 128
SUBLANES = 8
VMEM_LIMIT_BYTES = 56 * 1024 * 1024

ROW_TILE = 1024
MLP_ROW_TILE = 1024
FF_TILE = 1024
CONV_TIME_TILE = 512
CONV_TAIL = 32
CONV_ROW_CHUNK = 64
ATTN_TILE = 256

F32 = jnp.float32
BF16 = jnp.bfloat16


def _params(*semantics, flags=None):
    return pltpu.CompilerParams(dimension_semantics=semantics,
                                vmem_limit_bytes=VMEM_LIMIT_BYTES, flags=flags)


def _rms(x, g):
    return x * lax.rsqrt(jnp.mean(x * x, axis=-1, keepdims=True) + RMS_EPS) * g


def _dot(a, b):
    return jnp.dot(a, b, preferred_element_type=F32)


def _dot_nt(a, b):
    return lax.dot_general(a, b, (((1,), (1,)), ((), ())), preferred_element_type=F32)


def _full(shape):
    return pl.BlockSpec(shape, lambda *_: (0,) * len(shape))


def _pw1_glu_kernel(x_ref, g_ref, wa_ref, wb_ref, ba_ref, bb_ref, u_ref):
    hn = _rms(x_ref[...], g_ref[...]).astype(BF16)
    a = _dot(hn, wa_ref[...]) + ba_ref[...]
    b = _dot(hn, wb_ref[...]) + bb_ref[...]
    u_ref[...] = a * jax.nn.sigmoid(b)


def _pw1_glu(x, g, w, b):
    m, d = x.shape
    tm = min(ROW_TILE, m)
    row = pl.BlockSpec((tm, d), lambda i: (i, 0))
    half = lambda rows, h: pl.BlockSpec((rows, d), lambda i: (0, h))
    return pl.pallas_call(
        _pw1_glu_kernel,
        grid=(m // tm,),
        in_specs=[row, _full((1, d)), half(d, 0), half(d, 1), half(1, 0), half(1, 1)],
        out_specs=row,
        out_shape=jax.ShapeDtypeStruct((m, d), F32),
        compiler_params=_params("arbitrary"),
        name="pw1_glu",
    )(x, g, w, w, b, b)


def _conv_kernel(has_hist, u_ref, *refs):
    hist_ref = refs[0] if has_hist else None
    (x_ref, wdw_ref, bdw_ref, gln_ref, bln_ref, w2_ref, b2_ref,
     o_ref, win_ref, v_ref, slab_ref, shift_ref) = refs[1:] if has_hist else refs
    bn, tt, d = u_ref.shape
    rc = min(CONV_ROW_CHUNK, tt)
    n_row_chunks = tt // rc
    lead = CONV_TAIL - (CONV_WIDTH - 1)

    @pl.when(pl.program_id(1) == 0)
    def _():
        if has_hist:
            win_ref[:, 0:SUBLANES, :] = jnp.zeros((bn, SUBLANES, d), F32)
            win_ref[:, lead:CONV_TAIL, :] = hist_ref[...]
        else:
            win_ref[:, 0:CONV_TAIL, :] = jnp.zeros((bn, CONV_TAIL, d), F32)

    win_ref[:, CONV_TAIL:CONV_TAIL + tt, :] = u_ref[...]

    def chunk(idx, carry):
        b = idx // n_row_chunks
        r0 = pl.multiple_of((idx % n_row_chunks) * rc, SUBLANES)
        slab_ref[...] = win_ref[b, pl.ds(r0, rc + CONV_TAIL), :]
        for res in range(1, SUBLANES):
            shift_ref[res] = slab_ref[res:res + shift_ref.shape[1], :]
        for c in range(d // LANES):
            cols = slice(c * LANES, (c + 1) * LANES)
            acc = jnp.zeros((rc, LANES), F32)
            for k in range(CONV_WIDTH):
                res = (lead + k) % SUBLANES
                a = (lead + k) - res
                if res == 0:
                    xk = slab_ref[a:a + rc, cols]
                else:
                    xk = shift_ref[res, a:a + rc, cols]
                acc = acc + wdw_ref[k:k + 1, cols] * xk
            v_ref[b, pl.ds(r0, rc), cols] = acc
        return carry

    lax.fori_loop(0, bn * n_row_chunks, chunk, 0)

    win_ref[:, 0:CONV_TAIL, :] = win_ref[:, tt:tt + CONV_TAIL, :]

    v = v_ref[...].reshape(bn * tt, d) + bdw_ref[...]
    mu = jnp.mean(v, axis=-1, keepdims=True)
    vc = v - mu
    var = jnp.mean(vc * vc, axis=-1, keepdims=True)
    vn = vc * lax.rsqrt(var + LN_EPS) * gln_ref[...] + bln_ref[...]
    s = vn * jax.nn.sigmoid(vn)
    y = _dot(s.astype(BF16), w2_ref[...]) + b2_ref[...]
    o_ref[...] = x_ref[...] + y.reshape(bn, tt, d)


def _conv_tail(u, state, x, wdw, bdw, gln, bln, w2, b2, bn, tt):
    n, t, d = u.shape
    blk = pl.BlockSpec((bn, tt, d), lambda i, j: (i, j, 0))
    hist = [] if state is None else [state]
    hist_spec = [] if state is None else [
        pl.BlockSpec((None, bn, CONV_WIDTH - 1, d), lambda i, j: (0, i, 0, 0))]
    return pl.pallas_call(
        functools.partial(_conv_kernel, state is not None),
        grid=(n // bn, t // tt),
        in_specs=[blk] + hist_spec + [
                  blk,
                  _full((CONV_WIDTH, d)), _full((1, d)), _full((1, d)), _full((1, d)),
                  _full((d, d)), _full((1, d))],
        out_specs=blk,
        out_shape=jax.ShapeDtypeStruct((n, t, d), F32),
        scratch_shapes=[pltpu.VMEM((bn, CONV_TAIL + tt, d), F32),
                        pltpu.VMEM((bn, tt, d), F32),
                        pltpu.VMEM((min(CONV_ROW_CHUNK, tt) + CONV_TAIL, d), F32),
                        pltpu.VMEM((SUBLANES, min(CONV_ROW_CHUNK, tt) + CONV_TAIL - SUBLANES, d), F32)],
        compiler_params=_params("arbitrary", "arbitrary"),
        name="conv_tail",
    )(u, *hist, x, wdw, bdw, gln, bln, w2, b2)


def _mlp_kernel(final_norm, x_ref, g_ref, wup_ref, wdn_ref, gf_ref, o_ref, hn_ref, acc_ref):
    f = pl.program_id(1)

    @pl.when(f == 0)
    def _():
        x = x_ref[...]
        hn_ref[...] = _rms(x, g_ref[...]).astype(BF16)
        acc_ref[...] = x

    h = _dot(hn_ref[...], wup_ref[...])
    h = jnp.square(jnp.maximum(h, 0.0)).astype(BF16)
    acc_ref[...] += _dot(h, wdn_ref[...])

    @pl.when(f == pl.num_programs(1) - 1)
    def _():
        out = acc_ref[...]
        if final_norm:
            out = _rms(out, gf_ref[...])
        o_ref[...] = out


def _mlp(x, g, wup, wdn, layer, gf, final_norm):
    m, d = x.shape
    tm = min(MLP_ROW_TILE, m)
    row = pl.BlockSpec((tm, d), lambda i, f: (i, 0))
    return pl.pallas_call(
        functools.partial(_mlp_kernel, final_norm),
        grid=(m // tm, D_FF // FF_TILE),
        in_specs=[row, _full((1, d)),
                  pl.BlockSpec((None, d, FF_TILE), lambda i, f: (layer, 0, f)),
                  pl.BlockSpec((None, FF_TILE, d), lambda i, f: (layer, f, 0)),
                  _full((1, d))],
        out_specs=row,
        out_shape=jax.ShapeDtypeStruct((m, d), F32),
        scratch_shapes=[pltpu.VMEM((tm, d), BF16), pltpu.VMEM((tm, d), F32)],
        compiler_params=_params("arbitrary", "arbitrary"),
        name="mlp_final" if final_norm else "mlp",
    )(x, g, wup, wdn, gf)


KV_W_COLS = 4 * LANES


def _shared_kv_kernel(x_ref, g_ref, w_ref, gc_ref, cos_ref, sin_ref, ckv_ref, kpe_ref, kvb_ref):
    hn = _rms(x_ref[...], g_ref[...]).astype(BF16)
    a = _dot(hn, w_ref[...])
    ckv = _rms(a[:, :KV_LORA], gc_ref[...])
    kr = a[:, KV_LORA:KV_LORA + QK_ROPE_DIM]
    ks = a[:, KV_LORA + LANES:KV_LORA + LANES + QK_ROPE_DIM]
    kpe = kr * cos_ref[...] + ks * sin_ref[...]
    ckv_ref[...] = ckv
    kpe_ref[...] = kpe
    kvb_ref[:, :KV_LORA] = ckv.astype(BF16)
    kvb_ref[:, KV_LORA:] = kpe.astype(BF16)


def _shared_kv(x, g, w, gc, cos, sin):
    m, d = x.shape
    tm = min(ROW_TILE, m)
    n_pos = cos.shape[0] // tm
    row = lambda w_: pl.BlockSpec((tm, w_), lambda i: (i, 0))
    pos = pl.BlockSpec((tm, QK_ROPE_DIM), lambda i: (i % n_pos, 0))
    return pl.pallas_call(
        _shared_kv_kernel,
        grid=(m // tm,),
        in_specs=[row(d), _full((1, d)), _full((d, KV_W_COLS)), _full((1, KV_LORA)), pos, pos],
        out_specs=[row(KV_LORA), row(QK_ROPE_DIM), row(KV_DIM)],
        out_shape=[jax.ShapeDtypeStruct((m, KV_LORA), F32),
                   jax.ShapeDtypeStruct((m, QK_ROPE_DIM), F32),
                   jax.ShapeDtypeStruct((m, KV_DIM), BF16)],
        compiler_params=_params("arbitrary"),
        name="shared_kv",
    )(x, g, w, gc, cos, sin)


def _mla_q_kernel(x_ref, g_ref, wdq_ref, gq_ref, wuq_ref, wqr_ref, wqs_ref, wukt_ref,
                  cos_ref, sin_ref, q_ref):
    h = _rms(x_ref[...], g_ref[...]).astype(BF16)
    cq = _rms(_dot(h, wdq_ref[...]), gq_ref[...]).astype(BF16)
    qn = _dot(cq, wuq_ref[...]).astype(BF16)
    qpe = _dot(cq, wqr_ref[...]) * cos_ref[...] + _dot(cq, wqs_ref[...]) * sin_ref[...]
    for hd in range(N_HEADS):
        q_lat = _dot(qn[:, hd * QK_NOPE_DIM:(hd + 1) * QK_NOPE_DIM], wukt_ref[hd])
        q_ref[hd, :, :KV_LORA] = q_lat.astype(BF16)
        q_ref[hd, :, KV_LORA:] = qpe[:, hd * QK_ROPE_DIM:(hd + 1) * QK_ROPE_DIM].astype(BF16)


def _mla_q(x, g, wdq, gq, wuq, wqr, wqs, wukt, cos, sin):
    m, d = x.shape
    tm = min(ROW_TILE, m)
    n_pos = cos.shape[0] // tm
    rope_w = N_HEADS * QK_ROPE_DIM
    pos = pl.BlockSpec((tm, rope_w), lambda i: (i % n_pos, 0))
    return pl.pallas_call(
        _mla_q_kernel,
        grid=(m // tm,),
        in_specs=[pl.BlockSpec((tm, d), lambda i: (i, 0)), _full((1, d)),
                  _full((d, Q_LORA)), _full((1, Q_LORA)),
                  _full((Q_LORA, N_HEADS * QK_NOPE_DIM)),
                  _full((Q_LORA, rope_w)), _full((Q_LORA, rope_w)),
                  _full((N_HEADS, QK_NOPE_DIM, KV_LORA)), pos, pos],
        out_specs=pl.BlockSpec((N_HEADS, tm, KV_DIM), lambda i: (0, i, 0)),
        out_shape=jax.ShapeDtypeStruct((N_HEADS, m, KV_DIM), BF16),
        compiler_params=_params("arbitrary"),
        name="mla_q",
    )(x, g, wdq, gq, wuq, wqr, wqs, wukt, cos, sin)


def _lanes2(x):
    return jnp.concatenate([x] * (KV_LORA // LANES), axis=1)


def _prompt_attn_kernel(q_ref, kv_ref, o_ref, m_ref, l_ref, acc_ref):
    n_heads, tq, _ = q_ref.shape
    qi = pl.program_id(1)
    stat = (tq, LANES)

    q_all = q_ref[...].reshape(n_heads * tq, KV_DIM)

    def attend(kv, mask, first):
        s_all = _dot_nt(q_all, kv)
        ps, alphas = [], []
        for hd in range(n_heads):
            s = s_all[hd * tq:(hd + 1) * tq]
            if mask is not None:
                s = jnp.where(mask, s, -jnp.inf)
            m_cur = jnp.max(s, axis=-1, keepdims=True)
            if first:
                m_new = jnp.broadcast_to(m_cur, stat)
            else:
                m_prev = m_ref[hd]
                m_new = jnp.maximum(m_prev, m_cur)
            p = jnp.exp2((s - _lanes2(m_new)) * EXP2_SCALE)
            l_cur = jnp.sum(p, axis=-1, keepdims=True)
            ps.append(p.astype(BF16))
            if first:
                l_ref[hd] = jnp.broadcast_to(l_cur, stat)
            else:
                alpha = jnp.exp2((m_prev - m_new) * EXP2_SCALE)
                l_ref[hd] = alpha * l_ref[hd] + l_cur
                alphas.append(alpha)
            m_ref[hd] = m_new
        pv_all = _dot(jnp.concatenate(ps, axis=0), kv[:, :KV_LORA])
        for hd in range(n_heads):
            pv = pv_all[hd * tq:(hd + 1) * tq]
            if first:
                acc_ref[hd] = pv
            else:
                acc_ref[hd] = _lanes2(alphas[hd]) * acc_ref[hd] + pv

    k0 = pl.multiple_of(qi * tq, tq)
    t_idx = lax.broadcasted_iota(jnp.int32, (tq, tq), 0)
    c_idx = lax.broadcasted_iota(jnp.int32, (tq, tq), 1)
    attend(kv_ref[pl.ds(k0, tq), :], c_idx <= t_idx, True)

    def body(j, carry):
        kj = pl.multiple_of(j * tq, tq)
        attend(kv_ref[pl.ds(kj, tq), :], None, False)
        return carry

    lax.fori_loop(0, qi, body, 0)

    for hd in range(n_heads):
        o_ref[hd] = (acc_ref[hd] * _lanes2(1.0 / l_ref[hd])).astype(BF16)


def _prompt_attn(q, kvb, n, t):
    tq = ATTN_TILE
    nq = t // tq
    return pl.pallas_call(
        _prompt_attn_kernel,
        grid=(n, nq),
        in_specs=[pl.BlockSpec((N_HEADS, tq, KV_DIM), lambda b, i: (0, b * nq + i, 0)),
                  pl.BlockSpec((t, KV_DIM), lambda b, i: (b, 0))],
        out_specs=pl.BlockSpec((N_HEADS, tq, KV_LORA), lambda b, i: (0, b * nq + i, 0)),
        out_shape=jax.ShapeDtypeStruct((N_HEADS, n * t, KV_LORA), BF16),
        scratch_shapes=[pltpu.VMEM((N_HEADS, tq, LANES), F32),
                        pltpu.VMEM((N_HEADS, tq, LANES), F32),
                        pltpu.VMEM((N_HEADS, tq, KV_LORA), F32)],
        compiler_params=_params("arbitrary", "arbitrary"),
        name="prompt_attn",
    )(q, kvb)


NEW_ROWS_PAD = 16
CHUNK_PAGES = 32
RING_SLOTS = 4
DMA_UNROLL = 4


def _sample_attn_kernel(pt_ref, q_ref, new_ref, ckv_hbm, kpt_hbm, o_ref,
                        land_c, land_p, sem, kb_ref, kpt_ref, s_ref, p_ref):
    rows = q_ref.shape[1]
    n_new = rows // N_HEADS
    n_chunks = kb_ref.shape[0]
    page = land_c.shape[2]
    b = pl.program_id(0)
    total = pl.num_programs(0) * n_chunks
    ahead = RING_SLOTS - 1

    def page_copies(g, slot, i):
        pid = pt_ref[g * CHUNK_PAGES + i]
        return (pltpu.make_async_copy(ckv_hbm.at[pid], land_c.at[slot, i], sem.at[slot]),
                pltpu.make_async_copy(kpt_hbm.at[pid], land_p.at[slot, i], sem.at[slot]))

    def start_chunk(g):
        slot = g % RING_SLOTS

        def start_pair(j, carry):
            for parity in range(2):
                c_copy, p_copy = page_copies(g, slot, 2 * j + parity)
                c_copy.start(priority=parity)
                p_copy.start(priority=1 - parity)
            return carry

        lax.fori_loop(0, CHUNK_PAGES // 2, start_pair, 0, unroll=DMA_UNROLL // 2)

    def wait_chunk(g):
        slot = g % RING_SLOTS

        def wait_page(i, carry):
            for cp in page_copies(g, slot, i):
                cp.wait()
            return carry

        lax.fori_loop(0, CHUNK_PAGES, wait_page, 0, unroll=DMA_UNROLL)

    @pl.when(b == 0)
    def _():
        for g0 in range(ahead):
            start_chunk(g0)

    q = q_ref[0]
    q_lat, q_pe = q[:, :KV_LORA], q[:, KV_LORA:]

    def chunk_body(c, carry):
        g = b * n_chunks + c

        @pl.when(g + ahead < total)
        def _():
            start_chunk(g + ahead)

        wait_chunk(g)
        slot = g % RING_SLOTS
        for i in range(CHUNK_PAGES):
            kb_ref[c, i * page:(i + 1) * page, :] = land_c[slot, i].astype(BF16)
            kpt_ref[c, :, i * page:(i + 1) * page] = land_p[slot, i].astype(BF16)
        s_ref[c] = _dot_nt(q_lat, kb_ref[c]) + _dot(q_pe, kpt_ref[c])
        return carry

    lax.fori_loop(0, n_chunks, chunk_body, 0)

    kn = new_ref[0]
    s_new = _dot_nt(q, kn)
    t_idx = lax.broadcasted_iota(jnp.int32, s_new.shape, 0) % n_new
    c_idx = lax.broadcasted_iota(jnp.int32, s_new.shape, 1)
    s_new = jnp.where(c_idx <= t_idx, s_new, -jnp.inf)
    s_all = s_ref[...]
    m = jnp.maximum(jnp.max(jnp.max(s_all, axis=0), axis=-1, keepdims=True),
                    jnp.max(s_new, axis=-1, keepdims=True))
    p_all = jnp.exp2((s_all - m) * EXP2_SCALE)
    p_new = jnp.exp2((s_new - m) * EXP2_SCALE)
    l = (jnp.sum(jnp.sum(p_all, axis=0), axis=-1, keepdims=True)
         + jnp.sum(p_new, axis=-1, keepdims=True))
    p_ref[...] = p_all.astype(BF16)
    o = _dot(p_new.astype(BF16), kn[:, :KV_LORA])
    for c in range(n_chunks):
        o = o + _dot(p_ref[c], kb_ref[c])
    o_ref[0] = (o * (1.0 / l)).astype(BF16)


def _sample_attn(page_table, q, kv_new, cache_ckv, cache_kpt):
    n_seq, n_pages = page_table.shape
    page = cache_ckv.shape[1]
    rows = q.shape[1]
    n_chunks = n_pages // CHUNK_PAGES
    width = CHUNK_PAGES * page
    per_seq = lambda r, w: pl.BlockSpec((1, r, w), lambda b, pt: (b, 0, 0))
    grid_spec = pltpu.PrefetchScalarGridSpec(
        num_scalar_prefetch=1,
        grid=(n_seq,),
        in_specs=[per_seq(rows, KV_DIM), per_seq(NEW_ROWS_PAD, KV_DIM),
                  pl.BlockSpec(memory_space=pl.ANY), pl.BlockSpec(memory_space=pl.ANY)],
        out_specs=per_seq(rows, KV_LORA),
        scratch_shapes=[pltpu.VMEM((RING_SLOTS, CHUNK_PAGES, page, KV_LORA), F32),
                        pltpu.VMEM((RING_SLOTS, CHUNK_PAGES, QK_ROPE_DIM, page), F32),
                        pltpu.SemaphoreType.DMA((RING_SLOTS,)),
                        pltpu.VMEM((n_chunks, width, KV_LORA), BF16),
                        pltpu.VMEM((n_chunks, QK_ROPE_DIM, width), BF16),
                        pltpu.VMEM((n_chunks, rows, width), F32),
                        pltpu.VMEM((n_chunks, rows, width), BF16)])
    return pl.pallas_call(
        _sample_attn_kernel,
        grid_spec=grid_spec,
        out_shape=jax.ShapeDtypeStruct((n_seq, rows, KV_LORA), BF16),
        compiler_params=_params("arbitrary"),
        name="sample_attn",
    )(page_table.reshape(-1), q, kv_new, cache_ckv, cache_kpt)


def _attn_out_kernel(o_ref, x_ref, wuv_ref, wo_ref, y_ref):
    parts = [_dot(o_ref[hd], wuv_ref[hd]) for hd in range(N_HEADS)]
    o = jnp.concatenate(parts, axis=-1).astype(BF16)
    y_ref[...] = x_ref[...] + _dot(o, wo_ref[...])


def _attn_out(o_lat, x, wuv, wo):
    m, d = x.shape
    tm = min(ROW_TILE, m)
    row = pl.BlockSpec((tm, d), lambda i: (i, 0))
    return pl.pallas_call(
        _attn_out_kernel,
        grid=(m // tm,),
        in_specs=[pl.BlockSpec((N_HEADS, tm, KV_LORA), lambda i: (0, i, 0)), row,
                  _full((N_HEADS, KV_LORA, V_HEAD_DIM)),
                  _full((N_HEADS * V_HEAD_DIM, d))],
        out_specs=row,
        out_shape=jax.ShapeDtypeStruct((m, d), F32),
        compiler_params=_params("arbitrary"),
        name="attn_out",
    )(o_lat, x, wuv, wo)


def _rope_tables(pos):
    inv = 1.0 / (ROPE_BASE ** (jnp.arange(0, QK_ROPE_DIM, 2, dtype=F32) / QK_ROPE_DIM))
    ang = pos.astype(F32)[:, None] * inv[None, :]
    cos, sin = jnp.cos(ang), jnp.sin(ang)
    return jnp.concatenate([cos, cos], axis=-1), jnp.concatenate([-sin, sin], axis=-1)


def _half_swap_cols(w, n_groups):
    k = w.shape[0]
    half = QK_ROPE_DIM // 2
    return w.reshape(k, n_groups, 2, half)[:, :, ::-1, :].reshape(k, n_groups * QK_ROPE_DIM)


def kernel(x_prompt, x_sample, state_conv, cache_ckv, cache_kpe, page_table, g_mix, g_ffn, w_up, w_down, w_pw1, b_pw1, w_dw, b_dw, g_cln, b_cln, w_pw2, b_pw2, g_kv_in, w_dkv, g_ckv, w_kr, w_uk, w_uv, w_dq, g_q, w_uq, w_qr, w_o, g_final):
    d = D_MODEL
    row = lambda v: v.reshape(1, -1)

    w1, b1 = w_pw1[0].astype(BF16), row(b_pw1[0])
    w2 = w_pw2[0].astype(BF16)
    wup, wdn = w_up.astype(BF16), w_down.astype(BF16)
    pad = jnp.zeros((d, LANES - QK_ROPE_DIM), F32)
    wkv = jnp.concatenate([w_dkv, w_kr, pad, _half_swap_cols(w_kr, 1), pad], axis=1).astype(BF16)
    wdq, wuq = w_dq[0].astype(BF16), w_uq[0].astype(BF16)
    wqr, wqs = w_qr[0].astype(BF16), _half_swap_cols(w_qr[0], N_HEADS).astype(BF16)
    wukt = jnp.transpose(w_uk, (1, 2, 0)).astype(BF16)
    wuv = jnp.transpose(w_uv, (1, 0, 2)).astype(BF16)
    wo = w_o[0].astype(BF16)

    def dense_front(x, state, bn, tt, cos, sin):
        n, t, _ = x.shape
        xf = x.reshape(n * t, d)
        u = _pw1_glu(xf, row(g_mix[0]), w1, b1)
        t_pad = max(t, tt)
        u3, x3 = u.reshape(n, t, d), x
        if t_pad != t:
            zpad = ((0, 0), (0, t_pad - t), (0, 0))
            u3, x3 = jnp.pad(u3, zpad), jnp.pad(x3, zpad)
        x1 = _conv_tail(u3, state, x3, w_dw[0], row(b_dw[0]), row(g_cln[0]), row(b_cln[0]),
                        w2, row(b_pw2[0]), bn, tt)[:, :t].reshape(n * t, d)
        x2 = _mlp(x1, row(g_ffn[0]), wup, wdn, 0, row(g_final), False)
        ckv, kpe, kvb = _shared_kv(x2, row(g_kv_in), wkv, row(g_ckv), cos, sin)
        cos_h, sin_h = jnp.tile(cos, (1, N_HEADS)), jnp.tile(sin, (1, N_HEADS))
        q = _mla_q(x2, row(g_mix[1]), wdq, row(g_q[0]), wuq, wqr, wqs, wukt, cos_h, sin_h)
        return u, x2, ckv, kpe, kvb, q

    def dense_back(o_lat, x2):
        x3 = _attn_out(o_lat, x2, wuv, wo)
        return _mlp(x3, row(g_ffn[1]), wup, wdn, 1, row(g_final), True)

    n_p, t_p, _ = x_prompt.shape
    cos_p, sin_p = _rope_tables(jnp.arange(t_p))
    u_p, x2_p, ckv_p, kpe_p, kvb_p, q_p = dense_front(x_prompt, None, 1, CONV_TIME_TILE, cos_p, sin_p)
    o_p = _prompt_attn(q_p, kvb_p, n_p, t_p)
    y_p = dense_back(o_p, x2_p).reshape(n_p, t_p, d)
    conv_p = u_p.reshape(n_p, t_p, d)[:, t_p - (CONV_WIDTH - 1):][None]

    n_s, t_s, _ = x_sample.shape
    past_len = page_table.shape[1] * cache_ckv.shape[1]
    cos_s, sin_s = _rope_tables(past_len + jnp.arange(t_s))
    cos_s, sin_s = jnp.tile(cos_s, (n_s, 1)), jnp.tile(sin_s, (n_s, 1))
    bn_s = CONV_TIME_TILE // SUBLANES
    u_s, x2_s, ckv_s, kpe_s, kvb_s, q_s = dense_front(x_sample, state_conv, bn_s, SUBLANES, cos_s, sin_s)
    rows_s = N_HEADS * t_s
    q_seq = q_s.reshape(N_HEADS, n_s, t_s, KV_DIM).transpose(1, 0, 2, 3).reshape(n_s, rows_s, KV_DIM)
    kv_new = jnp.pad(kvb_s.reshape(n_s, t_s, KV_DIM), ((0, 0), (0, NEW_ROWS_PAD - t_s), (0, 0)))
    o_seq = _sample_attn(page_table, q_seq, kv_new, cache_ckv, jnp.swapaxes(cache_kpe, 1, 2))
    o_s = o_seq.reshape(n_s, N_HEADS, t_s, KV_LORA).transpose(1, 0, 2, 3).reshape(N_HEADS, n_s * t_s, KV_LORA)
    y_s = dense_back(o_s, x2_s).reshape(n_s, t_s, d)
    conv_s = jnp.concatenate([state_conv[0][:, t_s:], u_s.reshape(n_s, t_s, d)], axis=1)[None]

    return (y_p, y_s, conv_p,
            ckv_p.reshape(n_p, t_p, KV_LORA), kpe_p.reshape(n_p, t_p, QK_ROPE_DIM),
            conv_s,
            ckv_s.reshape(n_s, t_s, KV_LORA), kpe_s.reshape(n_s, t_s, QK_ROPE_DIM))
```

```python
import functools

import jax
import jax.numpy as jnp
from jax import lax
from jax.experimental import pallas as pl
from jax.experimental.pallas import tpu as pltpu

D_MODEL = 1024
D_FF = 4 * D_MODEL
CONV_WIDTH = 31
N_HEADS = 8
QK_NOPE_DIM = 128
QK_ROPE_DIM = 64
V_HEAD_DIM = 128
Q_LORA = 768
KV_LORA = 256
KV_DIM = KV_LORA + QK_ROPE_DIM
ROPE_BASE = 10000.0
RMS_EPS = 1e-6
LN_EPS = 1e-5
ATTN_SCALE = (QK_NOPE_DIM + QK_ROPE_DIM) ** -0.5
EXP2_SCALE = ATTN_SCALE * 1.4426950408889634

LANES = 128
SUBLANES = 8
VMEM_LIMIT_BYTES = 56 * 1024 * 1024

ROW_TILE = 1024
MLP_ROW_TILE = 1024
FF_TILE = 1024
CONV_TIME_TILE = 512
CONV_TAIL = 32
CONV_ROW_CHUNK = 64
ATTN_TILE = 256

F32 = jnp.float32
BF16 = jnp.bfloat16


def _params(*semantics, flags=None):
    return pltpu.CompilerParams(dimension_semantics=semantics,
                                vmem_limit_bytes=VMEM_LIMIT_BYTES, flags=flags)


def _rms(x, g):
    return x * lax.rsqrt(jnp.mean(x * x, axis=-1, keepdims=True) + RMS_EPS) * g


def _dot(a, b):
    return jnp.dot(a, b, preferred_element_type=F32)


def _dot_nt(a, b):
    return lax.dot_general(a, b, (((1,), (1,)), ((), ())), preferred_element_type=F32)


def _full(shape):
    return pl.BlockSpec(shape, lambda *_: (0,) * len(shape))


def _pw1_glu_kernel(x_ref, g_ref, wa_ref, wb_ref, ba_ref, bb_ref, u_ref):
    hn = _rms(x_ref[...], g_ref[...]).astype(BF16)
    a = _dot(hn, wa_ref[...]) + ba_ref[...]
    b = _dot(hn, wb_ref[...]) + bb_ref[...]
    u_ref[...] = a * jax.nn.sigmoid(b)


def _pw1_glu(x, g, w, b):
    m, d = x.shape
    tm = min(ROW_TILE, m)
    row = pl.BlockSpec((tm, d), lambda i: (i, 0))
    half = lambda rows, h: pl.BlockSpec((rows, d), lambda i: (0, h))
    return pl.pallas_call(
        _pw1_glu_kernel,
        grid=(m // tm,),
        in_specs=[row, _full((1, d)), half(d, 0), half(d, 1), half(1, 0), half(1, 1)],
        out_specs=row,
        out_shape=jax.ShapeDtypeStruct((m, d), F32),
        compiler_params=_params("arbitrary"),
        name="pw1_glu",
    )(x, g, w, w, b, b)


def _conv_kernel(has_hist, u_ref, *refs):
    hist_ref = refs[0] if has_hist else None
    (x_ref, wdw_ref, bdw_ref, gln_ref, bln_ref, w2_ref, b2_ref,
     o_ref, win_ref, v_ref, slab_ref, shift_ref) = refs[1:] if has_hist else refs
    bn, tt, d = u_ref.shape
    rc = min(CONV_ROW_CHUNK, tt)
    n_row_chunks = tt // rc
    lead = CONV_TAIL - (CONV_WIDTH - 1)

    @pl.when(pl.program_id(1) == 0)
    def _():
        if has_hist:
            win_ref[:, 0:SUBLANES, :] = jnp.zeros((bn, SUBLANES, d), F32)
            win_ref[:, lead:CONV_TAIL, :] = hist_ref[...]
        else:
            win_ref[:, 0:CONV_TAIL, :] = jnp.zeros((bn, CONV_TAIL, d), F32)

    win_ref[:, CONV_TAIL:CONV_TAIL + tt, :] = u_ref[...]

    def chunk(idx, carry):
        b = idx // n_row_chunks
        r0 = pl.multiple_of((idx % n_row_chunks) * rc, SUBLANES)
        slab_ref[...] = win_ref[b, pl.ds(r0, rc + CONV_TAIL), :]
        for res in range(1, SUBLANES):
            shift_ref[res] = slab_ref[res:res + shift_ref.shape[1], :]
        for c in range(d // LANES):
            cols = slice(c * LANES, (c + 1) * LANES)
            acc = jnp.zeros((rc, LANES), F32)
            for k in range(CONV_WIDTH):
                res = (lead + k) % SUBLANES
                a = (lead + k) - res
                if res == 0:
                    xk = slab_ref[a:a + rc, cols]
                else:
                    xk = shift_ref[res, a:a + rc, cols]
                acc = acc + wdw_ref[k:k + 1, cols] * xk
            v_ref[b, pl.ds(r0, rc), cols] = acc
        return carry

    lax.fori_loop(0, bn * n_row_chunks, chunk, 0)

    win_ref[:, 0:CONV_TAIL, :] = win_ref[:, tt:tt + CONV_TAIL, :]

    v = v_ref[...].reshape(bn * tt, d) + bdw_ref[...]
    mu = jnp.mean(v, axis=-1, keepdims=True)
    vc = v - mu
    var = jnp.mean(vc * vc, axis=-1, keepdims=True)
    vn = vc * lax.rsqrt(var + LN_EPS) * gln_ref[...] + bln_ref[...]
    s = vn * jax.nn.sigmoid(vn)
    y = _dot(s.astype(BF16), w2_ref[...]) + b2_ref[...]
    o_ref[...] = x_ref[...] + y.reshape(bn, tt, d)


def _conv_tail(u, state, x, wdw, bdw, gln, bln, w2, b2, bn, tt):
    n, t, d = u.shape
    blk = pl.BlockSpec((bn, tt, d), lambda i, j: (i, j, 0))
    hist = [] if state is None else [state]
    hist_spec = [] if state is None else [
        pl.BlockSpec((None, bn, CONV_WIDTH - 1, d), lambda i, j: (0, i, 0, 0))]
    return pl.pallas_call(
        functools.partial(_conv_kernel, state is not None),
        grid=(n // bn, t // tt),
        in_specs=[blk] + hist_spec + [
                  blk,
                  _full((CONV_WIDTH, d)), _full((1, d)), _full((1, d)), _full((1, d)),
                  _full((d, d)), _full((1, d))],
        out_specs=blk,
        out_shape=jax.ShapeDtypeStruct((n, t, d), F32),
        scratch_shapes=[pltpu.VMEM((bn, CONV_TAIL + tt, d), F32),
                        pltpu.VMEM((bn, tt, d), F32),
                        pltpu.VMEM((min(CONV_ROW_CHUNK, tt) + CONV_TAIL, d), F32),
                        pltpu.VMEM((SUBLANES, min(CONV_ROW_CHUNK, tt) + CONV_TAIL - SUBLANES, d), F32)],
        compiler_params=_params("arbitrary", "arbitrary"),
        name="conv_tail",
    )(u, *hist, x, wdw, bdw, gln, bln, w2, b2)


def _mlp_kernel(final_norm, x_ref, g_ref, wup_ref, wdn_ref, gf_ref, o_ref, hn_ref, acc_ref):
    f = pl.program_id(1)

    @pl.when(f == 0)
    def _():
        x = x_ref[...]
        hn_ref[...] = _rms(x, g_ref[...]).astype(BF16)
        acc_ref[...] = x

    h = _dot(hn_ref[...], wup_ref[...])
    h = jnp.square(jnp.maximum(h, 0.0)).astype(BF16)
    acc_ref[...] += _dot(h, wdn_ref[...])

    @pl.when(f == pl.num_programs(1) - 1)
    def _():
        out = acc_ref[...]
        if final_norm:
            out = _rms(out, gf_ref[...])
        o_ref[...] = out


def _mlp(x, g, wup, wdn, layer, gf, final_norm):
    m, d = x.shape
    tm = min(MLP_ROW_TILE, m)
    row = pl.BlockSpec((tm, d), lambda i, f: (i, 0))
    return pl.pallas_call(
        functools.partial(_mlp_kernel, final_norm),
        grid=(m // tm, D_FF // FF_TILE),
        in_specs=[row, _full((1, d)),
                  pl.BlockSpec((None, d, FF_TILE), lambda i, f: (layer, 0, f)),
                  pl.BlockSpec((None, FF_TILE, d), lambda i, f: (layer, f, 0)),
                  _full((1, d))],
        out_specs=row,
        out_shape=jax.ShapeDtypeStruct((m, d), F32),
        scratch_shapes=[pltpu.VMEM((tm, d), BF16), pltpu.VMEM((tm, d), F32)],
        compiler_params=_params("arbitrary", "arbitrary"),
        name="mlp_final" if final_norm else "mlp",
    )(x, g, wup, wdn, gf)


KV_W_COLS = 4 * LANES


def _shared_kv_kernel(x_ref, g_ref, w_ref, gc_ref, cos_ref, sin_ref, ckv_ref, kpe_ref, kvb_ref):
    hn = _rms(x_ref[...], g_ref[...]).astype(BF16)
    a = _dot(hn, w_ref[...])
    ckv = _rms(a[:, :KV_LORA], gc_ref[...])
    kr = a[:, KV_LORA:KV_LORA + QK_ROPE_DIM]
    ks = a[:, KV_LORA + LANES:KV_LORA + LANES + QK_ROPE_DIM]
    kpe = kr * cos_ref[...] + ks * sin_ref[...]
    ckv_ref[...] = ckv
    kpe_ref[...] = kpe
    kvb_ref[:, :KV_LORA] = ckv.astype(BF16)
    kvb_ref[:, KV_LORA:] = kpe.astype(BF16)


def _shared_kv(x, g, w, gc, cos, sin):
    m, d = x.shape
    tm = min(ROW_TILE, m)
    n_pos = cos.shape[0] // tm
    row = lambda w_: pl.BlockSpec((tm, w_), lambda i: (i, 0))
    pos = pl.BlockSpec((tm, QK_ROPE_DIM), lambda i: (i % n_pos, 0))
    return pl.pallas_call(
        _shared_kv_kernel,
        grid=(m // tm,),
        in_specs=[row(d), _full((1, d)), _full((d, KV_W_COLS)), _full((1, KV_LORA)), pos, pos],
        out_specs=[row(KV_LORA), row(QK_ROPE_DIM), row(KV_DIM)],
        out_shape=[jax.ShapeDtypeStruct((m, KV_LORA), F32),
                   jax.ShapeDtypeStruct((m, QK_ROPE_DIM), F32),
                   jax.ShapeDtypeStruct((m, KV_DIM), BF16)],
        compiler_params=_params("arbitrary"),
        name="shared_kv",
    )(x, g, w, gc, cos, sin)


def _mla_q_kernel(x_ref, g_ref, wdq_ref, gq_ref, wuq_ref, wqr_ref, wqs_ref, wukt_ref,
                  cos_ref, sin_ref, q_ref):
    h = _rms(x_ref[...], g_ref[...]).astype(BF16)
    cq = _rms(_dot(h, wdq_ref[...]), gq_ref[...]).astype(BF16)
    qn = _dot(cq, wuq_ref[...]).astype(BF16)
    qpe = _dot(cq, wqr_ref[...]) * cos_ref[...] + _dot(cq, wqs_ref[...]) * sin_ref[...]
    for hd in range(N_HEADS):
        q_lat = _dot(qn[:, hd * QK_NOPE_DIM:(hd + 1) * QK_NOPE_DIM], wukt_ref[hd])
        q_ref[hd, :, :KV_LORA] = q_lat.astype(BF16)
        q_ref[hd, :, KV_LORA:] = qpe[:, hd * QK_ROPE_DIM:(hd + 1) * QK_ROPE_DIM].astype(BF16)


def _mla_q(x, g, wdq, gq, wuq, wqr, wqs, wukt, cos, sin):
    m, d = x.shape
    tm = min(ROW_TILE, m)
    n_pos = cos.shape[0] // tm
    rope_w = N_HEADS * QK_ROPE_DIM
    pos = pl.BlockSpec((tm, rope_w), lambda i: (i % n_pos, 0))
    return pl.pallas_call(
        _mla_q_kernel,
        grid=(m // tm,),
        in_specs=[pl.BlockSpec((tm, d), lambda i: (i, 0)), _full((1, d)),
                  _full((d, Q_LORA)), _full((1, Q_LORA)),
                  _full((Q_LORA, N_HEADS * QK_NOPE_DIM)),
                  _full((Q_LORA, rope_w)), _full((Q_LORA, rope_w)),
                  _full((N_HEADS, QK_NOPE_DIM, KV_LORA)), pos, pos],
        out_specs=pl.BlockSpec((N_HEADS, tm, KV_DIM), lambda i: (0, i, 0)),
        out_shape=jax.ShapeDtypeStruct((N_HEADS, m, KV_DIM), BF16),
        compiler_params=_params("arbitrary"),
        name="mla_q",
    )(x, g, wdq, gq, wuq, wqr, wqs, wukt, cos, sin)


def _prompt_attn_kernel(q_ref, kv_ref, o_ref, s_ref, m_ref, mx_ref, l_ref, acc_ref):
    n_heads, tq, _ = q_ref.shape
    rows = n_heads * tq
    qi = pl.program_id(1)
    q_all = q_ref[...].reshape(rows, KV_DIM)

    def scores(j, mask):
        kj = pl.multiple_of(j * tq, tq)
        s = _dot_nt(q_all, kv_ref[pl.ds(kj, tq), :])
        if mask is not None:
            s = jnp.where(mask, s, -jnp.inf)
        s_ref[j] = s
        return s

    t_idx = lax.broadcasted_iota(jnp.int32, (n_heads, tq, tq), 1).reshape(rows, tq)
    c_idx = lax.broadcasted_iota(jnp.int32, (rows, tq), 1)
    m_ref[...] = scores(qi, c_idx <= t_idx)

    def score_tile(j, carry):
        m_ref[...] = jnp.maximum(m_ref[...], scores(j, None))
        return carry

    lax.fori_loop(0, qi, score_tile, 0)

    mx_ref[...] = jnp.broadcast_to(jnp.max(m_ref[...], axis=-1, keepdims=True), (rows, LANES))

    def probs(j):
        mx = mx_ref[...]
        p = jnp.exp2((s_ref[j] - jnp.concatenate([mx] * (tq // LANES), axis=1)) * EXP2_SCALE)
        kj = pl.multiple_of(j * tq, tq)
        pv = _dot(p.astype(BF16), kv_ref[pl.ds(kj, tq), :KV_LORA])
        return sum(p[:, i * LANES:(i + 1) * LANES] for i in range(tq // LANES)), pv

    l_ref[...], acc_ref[...] = probs(qi)

    def value_tile(j, carry):
        l_part, pv = probs(j)
        l_ref[...] += l_part
        acc_ref[...] += pv
        return carry

    lax.fori_loop(0, qi, value_tile, 0)

    inv_l = 1.0 / jnp.sum(l_ref[...], axis=-1, keepdims=True)
    o_ref[...] = (acc_ref[...] * inv_l).reshape(n_heads, tq, KV_LORA).astype(BF16)


def _prompt_attn(q, kvb, n, t):
    tq = ATTN_TILE
    nq = t // tq
    rows = N_HEADS * tq
    return pl.pallas_call(
        _prompt_attn_kernel,
        grid=(n, nq),
        in_specs=[pl.BlockSpec((N_HEADS, tq, KV_DIM), lambda b, i: (0, b * nq + i, 0)),
                  pl.BlockSpec((t, KV_DIM), lambda b, i: (b, 0))],
        out_specs=pl.BlockSpec((N_HEADS, tq, KV_LORA), lambda b, i: (0, b * nq + i, 0)),
        out_shape=jax.ShapeDtypeStruct((N_HEADS, n * t, KV_LORA), BF16),
        scratch_shapes=[pltpu.VMEM((nq, rows, tq), F32),
                        pltpu.VMEM((rows, tq), F32),
                        pltpu.VMEM((rows, LANES), F32),
                        pltpu.VMEM((rows, LANES), F32),
                        pltpu.VMEM((rows, KV_LORA), F32)],
        compiler_params=_params("arbitrary", "arbitrary"),
        name="prompt_attn",
    )(q, kvb)


NEW_ROWS_PAD = 16
CHUNK_PAGES = 32
RING_SLOTS = 4
DMA_UNROLL = 4


def _sample_attn_kernel(pt_ref, q_ref, new_ref, ckv_hbm, kpt_hbm, o_ref,
                        land_c, land_p, sem, kb_ref, kpt_ref, s_ref, p_ref):
    rows = q_ref.shape[1]
    n_new = rows // N_HEADS
    n_chunks = kb_ref.shape[0]
    page = land_c.shape[2]
    b = pl.program_id(0)
    total = pl.num_programs(0) * n_chunks
    ahead = RING_SLOTS - 1

    def page_copies(g, slot, i):
        pid = pt_ref[g * CHUNK_PAGES + i]
        return (pltpu.make_async_copy(ckv_hbm.at[pid], land_c.at[slot, i], sem.at[slot]),
                pltpu.make_async_copy(kpt_hbm.at[pid], land_p.at[slot, i], sem.at[slot]))

    def start_chunk(g):
        slot = g % RING_SLOTS

        def start_pair(j, carry):
            for parity in range(2):
                c_copy, p_copy = page_copies(g, slot, 2 * j + parity)
                c_copy.start(priority=parity)
                p_copy.start(priority=1 - parity)
            return carry

        lax.fori_loop(0, CHUNK_PAGES // 2, start_pair, 0, unroll=DMA_UNROLL // 2)

    def wait_chunk(g):
        slot = g % RING_SLOTS

        def wait_page(i, carry):
            for cp in page_copies(g, slot, i):
                cp.wait()
            return carry

        lax.fori_loop(0, CHUNK_PAGES, wait_page, 0, unroll=DMA_UNROLL)

    @pl.when(b == 0)
    def _():
        for g0 in range(ahead):
            start_chunk(g0)

    q = q_ref[0]
    q_lat, q_pe = q[:, :KV_LORA], q[:, KV_LORA:]

    def chunk_body(c, carry):
        g = b * n_chunks + c

        @pl.when(g + ahead < total)
        def _():
            start_chunk(g + ahead)

        wait_chunk(g)
        slot = g % RING_SLOTS
        for i in range(CHUNK_PAGES):
            kb_ref[c, i * page:(i + 1) * page, :] = land_c[slot, i].astype(BF16)
            kpt_ref[c, :, i * page:(i + 1) * page] = land_p[slot, i].astype(BF16)
        s_ref[c] = _dot_nt(q_lat, kb_ref[c]) + _dot(q_pe, kpt_ref[c])
        return carry

    lax.fori_loop(0, n_chunks, chunk_body, 0)

    kn = new_ref[0]
    s_new = _dot_nt(q, kn)
    t_idx = lax.broadcasted_iota(jnp.int32, s_new.shape, 0) % n_new
    c_idx = lax.broadcasted_iota(jnp.int32, s_new.shape, 1)
    s_new = jnp.where(c_idx <= t_idx, s_new, -jnp.inf)
    s_all = s_ref[...]
    m = jnp.maximum(jnp.max(jnp.max(s_all, axis=0), axis=-1, keepdims=True),
                    jnp.max(s_new, axis=-1, keepdims=True))
    p_all = jnp.exp2((s_all - m) * EXP2_SCALE)
    p_new = jnp.exp2((s_new - m) * EXP2_SCALE)
    l = (jnp.sum(jnp.sum(p_all, axis=0), axis=-1, keepdims=True)
         + jnp.sum(p_new, axis=-1, keepdims=True))
    p_ref[...] = p_all.astype(BF16)
    o = _dot(p_new.astype(BF16), kn[:, :KV_LORA])
    for c in range(n_chunks):
        o = o + _dot(p_ref[c], kb_ref[c])
    o_ref[0] = (o * (1.0 / l)).astype(BF16)


def _sample_attn(page_table, q, kv_new, cache_ckv, cache_kpt):
    n_seq, n_pages = page_table.shape
    page = cache_ckv.shape[1]
    rows = q.shape[1]
    n_chunks = n_pages // CHUNK_PAGES
    width = CHUNK_PAGES * page
    per_seq = lambda r, w: pl.BlockSpec((1, r, w), lambda b, pt: (b, 0, 0))
    grid_spec = pltpu.PrefetchScalarGridSpec(
        num_scalar_prefetch=1,
        grid=(n_seq,),
        in_specs=[per_seq(rows, KV_DIM), per_seq(NEW_ROWS_PAD, KV_DIM),
                  pl.BlockSpec(memory_space=pl.ANY), pl.BlockSpec(memory_space=pl.ANY)],
        out_specs=per_seq(rows, KV_LORA),
        scratch_shapes=[pltpu.VMEM((RING_SLOTS, CHUNK_PAGES, page, KV_LORA), F32),
                        pltpu.VMEM((RING_SLOTS, CHUNK_PAGES, QK_ROPE_DIM, page), F32),
                        pltpu.SemaphoreType.DMA((RING_SLOTS,)),
                        pltpu.VMEM((n_chunks, width, KV_LORA), BF16),
                        pltpu.VMEM((n_chunks, QK_ROPE_DIM, width), BF16),
                        pltpu.VMEM((n_chunks, rows, width), F32),
                        pltpu.VMEM((n_chunks, rows, width), BF16)])
    return pl.pallas_call(
        _sample_attn_kernel,
        grid_spec=grid_spec,
        out_shape=jax.ShapeDtypeStruct((n_seq, rows, KV_LORA), BF16),
        compiler_params=_params("arbitrary"),
        name="sample_attn",
    )(page_table.reshape(-1), q, kv_new, cache_ckv, cache_kpt)


def _attn_out_kernel(o_ref, x_ref, wuv_ref, wo_ref, y_ref):
    parts = [_dot(o_ref[hd], wuv_ref[hd]) for hd in range(N_HEADS)]
    o = jnp.concatenate(parts, axis=-1).astype(BF16)
    y_ref[...] = x_ref[...] + _dot(o, wo_ref[...])


def _attn_out(o_lat, x, wuv, wo):
    m, d = x.shape
    tm = min(ROW_TILE, m)
    row = pl.BlockSpec((tm, d), lambda i: (i, 0))
    return pl.pallas_call(
        _attn_out_kernel,
        grid=(m // tm,),
        in_specs=[pl.BlockSpec((N_HEADS, tm, KV_LORA), lambda i: (0, i, 0)), row,
                  _full((N_HEADS, KV_LORA, V_HEAD_DIM)),
                  _full((N_HEADS * V_HEAD_DIM, d))],
        out_specs=row,
        out_shape=jax.ShapeDtypeStruct((m, d), F32),
        compiler_params=_params("arbitrary"),
        name="attn_out",
    )(o_lat, x, wuv, wo)


def _rope_tables(pos):
    inv = 1.0 / (ROPE_BASE ** (jnp.arange(0, QK_ROPE_DIM, 2, dtype=F32) / QK_ROPE_DIM))
    ang = pos.astype(F32)[:, None] * inv[None, :]
    cos, sin = jnp.cos(ang), jnp.sin(ang)
    return jnp.concatenate([cos, cos], axis=-1), jnp.concatenate([-sin, sin], axis=-1)


def _half_swap_cols(w, n_groups):
    k = w.shape[0]
    half = QK_ROPE_DIM // 2
    return w.reshape(k, n_groups, 2, half)[:, :, ::-1, :].reshape(k, n_groups * QK_ROPE_DIM)


def kernel(x_prompt, x_sample, state_conv, cache_ckv, cache_kpe, page_table, g_mix, g_ffn, w_up, w_down, w_pw1, b_pw1, w_dw, b_dw, g_cln, b_cln, w_pw2, b_pw2, g_kv_in, w_dkv, g_ckv, w_kr, w_uk, w_uv, w_dq, g_q, w_uq, w_qr, w_o, g_final):
    d = D_MODEL
    row = lambda v: v.reshape(1, -1)

    w1, b1 = w_pw1[0].astype(BF16), row(b_pw1[0])
    w2 = w_pw2[0].astype(BF16)
    wup, wdn = w_up.astype(BF16), w_down.astype(BF16)
    pad = jnp.zeros((d, LANES - QK_ROPE_DIM), F32)
    wkv = jnp.concatenate([w_dkv, w_kr, pad, _half_swap_cols(w_kr, 1), pad], axis=1).astype(BF16)
    wdq, wuq = w_dq[0].astype(BF16), w_uq[0].astype(BF16)
    wqr, wqs = w_qr[0].astype(BF16), _half_swap_cols(w_qr[0], N_HEADS).astype(BF16)
    wukt = jnp.transpose(w_uk, (1, 2, 0)).astype(BF16)
    wuv = jnp.transpose(w_uv, (1, 0, 2)).astype(BF16)
    wo = w_o[0].astype(BF16)

    def dense_front(x, state, bn, tt, cos, sin):
        n, t, _ = x.shape
        xf = x.reshape(n * t, d)
        u = _pw1_glu(xf, row(g_mix[0]), w1, b1)
        t_pad = max(t, tt)
        u3, x3 = u.reshape(n, t, d), x
        if t_pad != t:
            zpad = ((0, 0), (0, t_pad - t), (0, 0))
            u3, x3 = jnp.pad(u3, zpad), jnp.pad(x3, zpad)
        x1 = _conv_tail(u3, state, x3, w_dw[0], row(b_dw[0]), row(g_cln[0]), row(b_cln[0]),
                        w2, row(b_pw2[0]), bn, tt)[:, :t].reshape(n * t, d)
        x2 = _mlp(x1, row(g_ffn[0]), wup, wdn, 0, row(g_final), False)
        ckv, kpe, kvb = _shared_kv(x2, row(g_kv_in), wkv, row(g_ckv), cos, sin)
        cos_h, sin_h = jnp.tile(cos, (1, N_HEADS)), jnp.tile(sin, (1, N_HEADS))
        q = _mla_q(x2, row(g_mix[1]), wdq, row(g_q[0]), wuq, wqr, wqs, wukt, cos_h, sin_h)
        return u, x2, ckv, kpe, kvb, q

    def dense_back(o_lat, x2):
        x3 = _attn_out(o_lat, x2, wuv, wo)
        return _mlp(x3, row(g_ffn[1]), wup, wdn, 1, row(g_final), True)

    n_p, t_p, _ = x_prompt.shape
    cos_p, sin_p = _rope_tables(jnp.arange(t_p))
    u_p, x2_p, ckv_p, kpe_p, kvb_p, q_p = dense_front(x_prompt, None, 1, CONV_TIME_TILE, cos_p, sin_p)
    o_p = _prompt_attn(q_p, kvb_p, n_p, t_p)
    y_p = dense_back(o_p, x2_p).reshape(n_p, t_p, d)
    conv_p = u_p.reshape(n_p, t_p, d)[:, t_p - (CONV_WIDTH - 1):][None]

    n_s, t_s, _ = x_sample.shape
    past_len = page_table.shape[1] * cache_ckv.shape[1]
    cos_s, sin_s = _rope_tables(past_len + jnp.arange(t_s))
    cos_s, sin_s = jnp.tile(cos_s, (n_s, 1)), jnp.tile(sin_s, (n_s, 1))
    bn_s = CONV_TIME_TILE // SUBLANES
    u_s, x2_s, ckv_s, kpe_s, kvb_s, q_s = dense_front(x_sample, state_conv, bn_s, SUBLANES, cos_s, sin_s)
    rows_s = N_HEADS * t_s
    q_seq = q_s.reshape(N_HEADS, n_s, t_s, KV_DIM).transpose(1, 0, 2, 3).reshape(n_s, rows_s, KV_DIM)
    kv_new = jnp.pad(kvb_s.reshape(n_s, t_s, KV_DIM), ((0, 0), (0, NEW_ROWS_PAD - t_s), (0, 0)))
    o_seq = _sample_attn(page_table, q_seq, kv_new, cache_ckv, jnp.swapaxes(cache_kpe, 1, 2))
    o_s = o_seq.reshape(n_s, N_HEADS, t_s, KV_LORA).transpose(1, 0, 2, 3).reshape(N_HEADS, n_s * t_s, KV_LORA)
    y_s = dense_back(o_s, x2_s).reshape(n_s, t_s, d)
    conv_s = jnp.concatenate([state_conv[0][:, t_s:], u_s.reshape(n_s, t_s, d)], axis=1)[None]

    return (y_p, y_s, conv_p,
            ckv_p.reshape(n_p, t_p, KV_LORA), kpe_p.reshape(n_p, t_p, QK_ROPE_DIM),
            conv_s,
            ckv_s.reshape(n_s, t_s, KV_LORA), kpe_s.reshape(n_s, t_s, QK_ROPE_DIM))
```

```python
import functools

import jax
import jax.numpy as jnp
from jax import lax
from jax.experimental import pallas as pl
from jax.experimental.pallas import tpu as pltpu

D_MODEL = 1024
D_FF = 4 * D_MODEL
CONV_WIDTH = 31
N_HEADS = 8
QK_NOPE_DIM = 128
QK_ROPE_DIM = 64
V_HEAD_DIM = 128
Q_LORA = 768
KV_LORA = 256
KV_DIM = KV_LORA + QK_ROPE_DIM
ROPE_BASE = 10000.0
RMS_EPS = 1e-6
LN_EPS = 1e-5
ATTN_SCALE = (QK_NOPE_DIM + QK_ROPE_DIM) ** -0.5
EXP2_SCALE = ATTN_SCALE * 1.4426950408889634

LANES = 128
SUBLANES = 8
VMEM_LIMIT_BYTES = 56 * 1024 * 1024

ROW_TILE = 1024
MLP_ROW_TILE = 1024
FF_TILE = 1024
CONV_TIME_TILE = 512
CONV_TAIL = 32
CONV_ROW_CHUNK = 64
ATTN_TILE = 256

F32 = jnp.float32
BF16 = jnp.bfloat16


def _params(*semantics, flags=None):
    return pltpu.CompilerParams(dimension_semantics=semantics,
                                vmem_limit_bytes=VMEM_LIMIT_BYTES, flags=flags)


def _rms(x, g):
    return x * lax.rsqrt(jnp.mean(x * x, axis=-1, keepdims=True) + RMS_EPS) * g


def _dot(a, b):
    return jnp.dot(a, b, preferred_element_type=F32)


def _dot_nt(a, b):
    return lax.dot_general(a, b, (((1,), (1,)), ((), ())), preferred_element_type=F32)


def _full(shape):
    return pl.BlockSpec(shape, lambda *_: (0,) * len(shape))


def _pw1_glu_kernel(x_ref, g_ref, wa_ref, wb_ref, ba_ref, bb_ref, u_ref):
    hn = _rms(x_ref[...], g_ref[...]).astype(BF16)
    a = _dot(hn, wa_ref[...]) + ba_ref[...]
    b = _dot(hn, wb_ref[...]) + bb_ref[...]
    u_ref[...] = a * jax.nn.sigmoid(b)


def _pw1_glu(x, g, w, b):
    m, d = x.shape
    tm = min(ROW_TILE, m)
    row = pl.BlockSpec((tm, d), lambda i: (i, 0))
    half = lambda rows, h: pl.BlockSpec((rows, d), lambda i: (0, h))
    return pl.pallas_call(
        _pw1_glu_kernel,
        grid=(m // tm,),
        in_specs=[row, _full((1, d)), half(d, 0), half(d, 1), half(1, 0), half(1, 1)],
        out_specs=row,
        out_shape=jax.ShapeDtypeStruct((m, d), F32),
        compiler_params=_params("arbitrary"),
        name="pw1_glu",
    )(x, g, w, w, b, b)


def _conv_kernel(has_hist, u_ref, *refs):
    hist_ref = refs[0] if has_hist else None
    (x_ref, wdw_ref, bdw_ref, gln_ref, bln_ref, w2_ref, b2_ref,
     o_ref, win_ref, v_ref, slab_ref, shift_ref) = refs[1:] if has_hist else refs
    bn, tt, d = u_ref.shape
    rc = min(CONV_ROW_CHUNK, tt)
    n_row_chunks = tt // rc
    lead = CONV_TAIL - (CONV_WIDTH - 1)

    @pl.when(pl.program_id(1) == 0)
    def _():
        if has_hist:
            win_ref[:, 0:SUBLANES, :] = jnp.zeros((bn, SUBLANES, d), F32)
            win_ref[:, lead:CONV_TAIL, :] = hist_ref[...]
        else:
            win_ref[:, 0:CONV_TAIL, :] = jnp.zeros((bn, CONV_TAIL, d), F32)

    win_ref[:, CONV_TAIL:CONV_TAIL + tt, :] = u_ref[...]

    def chunk(idx, carry):
        b = idx // n_row_chunks
        r0 = pl.multiple_of((idx % n_row_chunks) * rc, SUBLANES)
        slab_ref[...] = win_ref[b, pl.ds(r0, rc + CONV_TAIL), :]
        for res in range(1, SUBLANES):
            shift_ref[res] = slab_ref[res:res + shift_ref.shape[1], :]
        for c in range(d // LANES):
            cols = slice(c * LANES, (c + 1) * LANES)
            acc = jnp.zeros((rc, LANES), F32)
            for k in range(CONV_WIDTH):
                res = (lead + k) % SUBLANES
                a = (lead + k) - res
                if res == 0:
                    xk = slab_ref[a:a + rc, cols]
                else:
                    xk = shift_ref[res, a:a + rc, cols]
                acc = acc + wdw_ref[k:k + 1, cols] * xk
            v_ref[b, pl.ds(r0, rc), cols] = acc
        return carry

    lax.fori_loop(0, bn * n_row_chunks, chunk, 0)

    win_ref[:, 0:CONV_TAIL, :] = win_ref[:, tt:tt + CONV_TAIL, :]

    v = v_ref[...].reshape(bn * tt, d) + bdw_ref[...]
    mu = jnp.mean(v, axis=-1, keepdims=True)
    vc = v - mu
    var = jnp.mean(vc * vc, axis=-1, keepdims=True)
    vn = vc * lax.rsqrt(var + LN_EPS) * gln_ref[...] + bln_ref[...]
    s = vn * jax.nn.sigmoid(vn)
    y = _dot(s.astype(BF16), w2_ref[...]) + b2_ref[...]
    o_ref[...] = x_ref[...] + y.reshape(bn, tt, d)


def _conv_tail(u, state, x, wdw, bdw, gln, bln, w2, b2, bn, tt):
    n, t, d = u.shape
    blk = pl.BlockSpec((bn, tt, d), lambda i, j: (i, j, 0))
    hist = [] if state is None else [state]
    hist_spec = [] if state is None else [
        pl.BlockSpec((None, bn, CONV_WIDTH - 1, d), lambda i, j: (0, i, 0, 0))]
    return pl.pallas_call(
        functools.partial(_conv_kernel, state is not None),
        grid=(n // bn, t // tt),
        in_specs=[blk] + hist_spec + [
                  blk,
                  _full((CONV_WIDTH, d)), _full((1, d)), _full((1, d)), _full((1, d)),
                  _full((d, d)), _full((1, d))],
        out_specs=blk,
        out_shape=jax.ShapeDtypeStruct((n, t, d), F32),
        scratch_shapes=[pltpu.VMEM((bn, CONV_TAIL + tt, d), F32),
                        pltpu.VMEM((bn, tt, d), F32),
                        pltpu.VMEM((min(CONV_ROW_CHUNK, tt) + CONV_TAIL, d), F32),
                        pltpu.VMEM((SUBLANES, min(CONV_ROW_CHUNK, tt) + CONV_TAIL - SUBLANES, d), F32)],
        compiler_params=_params("arbitrary", "arbitrary"),
        name="conv_tail",
    )(u, *hist, x, wdw, bdw, gln, bln, w2, b2)


def _mlp_kernel(final_norm, x_ref, g_ref, wup_ref, wdn_ref, gf_ref, o_ref, hn_ref, acc_ref):
    f = pl.program_id(1)

    @pl.when(f == 0)
    def _():
        x = x_ref[...]
        hn_ref[...] = _rms(x, g_ref[...]).astype(BF16)
        acc_ref[...] = x

    h = _dot(hn_ref[...], wup_ref[...])
    h = jnp.square(jnp.maximum(h, 0.0)).astype(BF16)
    acc_ref[...] += _dot(h, wdn_ref[...])

    @pl.when(f == pl.num_programs(1) - 1)
    def _():
        out = acc_ref[...]
        if final_norm:
            out = _rms(out, gf_ref[...])
        o_ref[...] = out


def _mlp(x, g, wup, wdn, layer, gf, final_norm):
    m, d = x.shape
    tm = min(MLP_ROW_TILE, m)
    row = pl.BlockSpec((tm, d), lambda i, f: (i, 0))
    return pl.pallas_call(
        functools.partial(_mlp_kernel, final_norm),
        grid=(m // tm, D_FF // FF_TILE),
        in_specs=[row, _full((1, d)),
                  pl.BlockSpec((None, d, FF_TILE), lambda i, f: (layer, 0, f)),
                  pl.BlockSpec((None, FF_TILE, d), lambda i, f: (layer, f, 0)),
                  _full((1, d))],
        out_specs=row,
        out_shape=jax.ShapeDtypeStruct((m, d), F32),
        scratch_shapes=[pltpu.VMEM((tm, d), BF16), pltpu.VMEM((tm, d), F32)],
        compiler_params=_params("arbitrary", "arbitrary"),
        name="mlp_final" if final_norm else "mlp",
    )(x, g, wup, wdn, gf)


KV_W_COLS = 4 * LANES


def _latent_qkv_kernel(x_ref, gkv_ref, wkv_ref, gc_ref, cosk_ref, sink_ref,
                       gq_in_ref, wdq_ref, gq_ref, wuq_ref, wqr_ref, wqs_ref, wukt_ref,
                       cosq_ref, sinq_ref, ckv_ref, kpe_ref, kvb_ref, q_ref):
    x = x_ref[...]
    xn = x * lax.rsqrt(jnp.mean(x * x, axis=-1, keepdims=True) + RMS_EPS)

    a = _dot((xn * gkv_ref[...]).astype(BF16), wkv_ref[...])
    ckv = _rms(a[:, :KV_LORA], gc_ref[...])
    kr = a[:, KV_LORA:KV_LORA + QK_ROPE_DIM]
    ks = a[:, KV_LORA + LANES:KV_LORA + LANES + QK_ROPE_DIM]
    kpe = kr * cosk_ref[...] + ks * sink_ref[...]
    ckv_ref[...] = ckv
    kpe_ref[...] = kpe
    kvb_ref[:, :KV_LORA] = ckv.astype(BF16)
    kvb_ref[:, KV_LORA:] = kpe.astype(BF16)

    h = (xn * gq_in_ref[...]).astype(BF16)
    cq = _rms(_dot(h, wdq_ref[...]), gq_ref[...]).astype(BF16)
    qn = _dot(cq, wuq_ref[...]).astype(BF16)
    qpe = _dot(cq, wqr_ref[...]) * cosq_ref[...] + _dot(cq, wqs_ref[...]) * sinq_ref[...]
    for hd in range(N_HEADS):
        q_lat = _dot(qn[:, hd * QK_NOPE_DIM:(hd + 1) * QK_NOPE_DIM], wukt_ref[hd])
        q_ref[hd, :, :KV_LORA] = q_lat.astype(BF16)
        q_ref[hd, :, KV_LORA:] = qpe[:, hd * QK_ROPE_DIM:(hd + 1) * QK_ROPE_DIM].astype(BF16)


def _latent_qkv(x, gkv, wkv, gc, cosk, sink, gq_in, wdq, gq, wuq, wqr, wqs, wukt, cosq, sinq):
    m, d = x.shape
    tm = min(ROW_TILE, m)
    n_pos = cosk.shape[0] // tm
    rope_w = N_HEADS * QK_ROPE_DIM
    row = lambda w_: pl.BlockSpec((tm, w_), lambda i: (i, 0))
    pos = lambda w_: pl.BlockSpec((tm, w_), lambda i: (i % n_pos, 0))
    return pl.pallas_call(
        _latent_qkv_kernel,
        grid=(m // tm,),
        in_specs=[row(d), _full((1, d)), _full((d, KV_W_COLS)), _full((1, KV_LORA)),
                  pos(QK_ROPE_DIM), pos(QK_ROPE_DIM),
                  _full((1, d)), _full((d, Q_LORA)), _full((1, Q_LORA)),
                  _full((Q_LORA, N_HEADS * QK_NOPE_DIM)),
                  _full((Q_LORA, rope_w)), _full((Q_LORA, rope_w)),
                  _full((N_HEADS, QK_NOPE_DIM, KV_LORA)), pos(rope_w), pos(rope_w)],
        out_specs=[row(KV_LORA), row(QK_ROPE_DIM), row(KV_DIM),
                   pl.BlockSpec((N_HEADS, tm, KV_DIM), lambda i: (0, i, 0))],
        out_shape=[jax.ShapeDtypeStruct((m, KV_LORA), F32),
                   jax.ShapeDtypeStruct((m, QK_ROPE_DIM), F32),
                   jax.ShapeDtypeStruct((m, KV_DIM), BF16),
                   jax.ShapeDtypeStruct((N_HEADS, m, KV_DIM), BF16)],
        compiler_params=_params("arbitrary"),
        name="latent_qkv",
    )(x, gkv, wkv, gc, cosk, sink, gq_in, wdq, gq, wuq, wqr, wqs, wukt, cosq, sinq)


def _prompt_attn_kernel(q_ref, kv_ref, o_ref, s_ref, m_ref, mx_ref, l_ref, acc_ref):
    n_heads, tq, _ = q_ref.shape
    rows = n_heads * tq
    qi = pl.program_id(1)
    q_all = q_ref[...].reshape(rows, KV_DIM)

    def scores(j, mask):
        kj = pl.multiple_of(j * tq, tq)
        s = _dot_nt(q_all, kv_ref[pl.ds(kj, tq), :])
        if mask is not None:
            s = jnp.where(mask, s, -jnp.inf)
        s_ref[j] = s
        return s

    t_idx = lax.broadcasted_iota(jnp.int32, (n_heads, tq, tq), 1).reshape(rows, tq)
    c_idx = lax.broadcasted_iota(jnp.int32, (rows, tq), 1)
    m_ref[...] = scores(qi, c_idx <= t_idx)

    def score_tile(j, carry):
        m_ref[...] = jnp.maximum(m_ref[...], scores(j, None))
        return carry

    lax.fori_loop(0, qi, score_tile, 0)

    mx_ref[...] = jnp.broadcast_to(jnp.max(m_ref[...], axis=-1, keepdims=True), (rows, LANES))

    def probs(j):
        mx = mx_ref[...]
        p = jnp.exp2((s_ref[j] - jnp.concatenate([mx] * (tq // LANES), axis=1)) * EXP2_SCALE)
        kj = pl.multiple_of(j * tq, tq)
        pv = _dot(p.astype(BF16), kv_ref[pl.ds(kj, tq), :KV_LORA])
        return sum(p[:, i * LANES:(i + 1) * LANES] for i in range(tq // LANES)), pv

    l_ref[...], acc_ref[...] = probs(qi)

    def value_tile(j, carry):
        l_part, pv = probs(j)
        l_ref[...] += l_part
        acc_ref[...] += pv
        return carry

    lax.fori_loop(0, qi, value_tile, 0)

    inv_l = 1.0 / jnp.sum(l_ref[...], axis=-1, keepdims=True)
    o_ref[...] = (acc_ref[...] * inv_l).reshape(n_heads, tq, KV_LORA).astype(BF16)


def _prompt_attn(q, kvb, n, t):
    tq = ATTN_TILE
    nq = t // tq
    rows = N_HEADS * tq
    return pl.pallas_call(
        _prompt_attn_kernel,
        grid=(n, nq),
        in_specs=[pl.BlockSpec((N_HEADS, tq, KV_DIM), lambda b, i: (0, b * nq + i, 0)),
                  pl.BlockSpec((t, KV_DIM), lambda b, i: (b, 0))],
        out_specs=pl.BlockSpec((N_HEADS, tq, KV_LORA), lambda b, i: (0, b * nq + i, 0)),
        out_shape=jax.ShapeDtypeStruct((N_HEADS, n * t, KV_LORA), BF16),
        scratch_shapes=[pltpu.VMEM((nq, rows, tq), F32),
                        pltpu.VMEM((rows, tq), F32),
                        pltpu.VMEM((rows, LANES), F32),
                        pltpu.VMEM((rows, LANES), F32),
                        pltpu.VMEM((rows, KV_LORA), F32)],
        compiler_params=_params("arbitrary", "arbitrary"),
        name="prompt_attn",
    )(q, kvb)


NEW_ROWS_PAD = 16
CHUNK_PAGES = 32
RING_SLOTS = 4
DMA_UNROLL = 4


def _sample_attn_kernel(pt_ref, q_ref, new_ref, ckv_hbm, kpt_hbm, o_ref,
                        land_c, land_p, sem, kb_ref, kpt_ref, s_ref, p_ref):
    rows = q_ref.shape[1]
    n_new = rows // N_HEADS
    n_chunks = kb_ref.shape[0]
    page = land_c.shape[2]
    b = pl.program_id(0)
    total = pl.num_programs(0) * n_chunks
    ahead = RING_SLOTS - 1

    def page_copies(g, slot, i):
        pid = pt_ref[g * CHUNK_PAGES + i]
        return (pltpu.make_async_copy(ckv_hbm.at[pid], land_c.at[slot, i], sem.at[slot]),
                pltpu.make_async_copy(kpt_hbm.at[pid], land_p.at[slot, i], sem.at[slot]))

    def start_chunk(g):
        slot = g % RING_SLOTS

        def start_pair(j, carry):
            for parity in range(2):
                c_copy, p_copy = page_copies(g, slot, 2 * j + parity)
                c_copy.start(priority=parity)
                p_copy.start(priority=1 - parity)
            return carry

        lax.fori_loop(0, CHUNK_PAGES // 2, start_pair, 0, unroll=DMA_UNROLL // 2)

    def wait_chunk(g):
        slot = g % RING_SLOTS

        def wait_page(i, carry):
            for cp in page_copies(g, slot, i):
                cp.wait()
            return carry

        lax.fori_loop(0, CHUNK_PAGES, wait_page, 0, unroll=DMA_UNROLL)

    @pl.when(b == 0)
    def _():
        for g0 in range(ahead):
            start_chunk(g0)

    q = q_ref[0]
    q_lat, q_pe = q[:, :KV_LORA], q[:, KV_LORA:]

    def chunk_body(c, carry):
        g = b * n_chunks + c

        @pl.when(g + ahead < total)
        def _():
            start_chunk(g + ahead)

        wait_chunk(g)
        slot = g % RING_SLOTS
        for i in range(CHUNK_PAGES):
            kb_ref[c, i * page:(i + 1) * page, :] = land_c[slot, i].astype(BF16)
            kpt_ref[c, :, i * page:(i + 1) * page] = land_p[slot, i].astype(BF16)
        s_ref[c] = _dot_nt(q_lat, kb_ref[c]) + _dot(q_pe, kpt_ref[c])
        return carry

    lax.fori_loop(0, n_chunks, chunk_body, 0)

    kn = new_ref[0]
    s_new = _dot_nt(q, kn)
    t_idx = lax.broadcasted_iota(jnp.int32, s_new.shape, 0) % n_new
    c_idx = lax.broadcasted_iota(jnp.int32, s_new.shape, 1)
    s_new = jnp.where(c_idx <= t_idx, s_new, -jnp.inf)
    s_all = s_ref[...]
    m = jnp.maximum(jnp.max(jnp.max(s_all, axis=0), axis=-1, keepdims=True),
                    jnp.max(s_new, axis=-1, keepdims=True))
    p_all = jnp.exp2((s_all - m) * EXP2_SCALE)
    p_new = jnp.exp2((s_new - m) * EXP2_SCALE)
    l = (jnp.sum(jnp.sum(p_all, axis=0), axis=-1, keepdims=True)
         + jnp.sum(p_new, axis=-1, keepdims=True))
    p_ref[...] = p_all.astype(BF16)
    o = _dot(p_new.astype(BF16), kn[:, :KV_LORA])
    for c in range(n_chunks):
        o = o + _dot(p_ref[c], kb_ref[c])
    o_ref[0] = (o * (1.0 / l)).astype(BF16)


def _sample_attn(page_table, q, kv_new, cache_ckv, cache_kpt):
    n_seq, n_pages = page_table.shape
    page = cache_ckv.shape[1]
    rows = q.shape[1]
    n_chunks = n_pages // CHUNK_PAGES
    width = CHUNK_PAGES * page
    per_seq = lambda r, w: pl.BlockSpec((1, r, w), lambda b, pt: (b, 0, 0))
    grid_spec = pltpu.PrefetchScalarGridSpec(
        num_scalar_prefetch=1,
        grid=(n_seq,),
        in_specs=[per_seq(rows, KV_DIM), per_seq(NEW_ROWS_PAD, KV_DIM),
                  pl.BlockSpec(memory_space=pl.ANY), pl.BlockSpec(memory_space=pl.ANY)],
        out_specs=per_seq(rows, KV_LORA),
        scratch_shapes=[pltpu.VMEM((RING_SLOTS, CHUNK_PAGES, page, KV_LORA), F32),
                        pltpu.VMEM((RING_SLOTS, CHUNK_PAGES, QK_ROPE_DIM, page), F32),
                        pltpu.SemaphoreType.DMA((RING_SLOTS,)),
                        pltpu.VMEM((n_chunks, width, KV_LORA), BF16),
                        pltpu.VMEM((n_chunks, QK_ROPE_DIM, width), BF16),
                        pltpu.VMEM((n_chunks, rows, width), F32),
                        pltpu.VMEM((n_chunks, rows, width), BF16)])
    return pl.pallas_call(
        _sample_attn_kernel,
        grid_spec=grid_spec,
        out_shape=jax.ShapeDtypeStruct((n_seq, rows, KV_LORA), BF16),
        compiler_params=_params("arbitrary"),
        name="sample_attn",
    )(page_table.reshape(-1), q, kv_new, cache_ckv, cache_kpt)


def _attn_out_kernel(o_ref, x_ref, wuv_ref, wo_ref, y_ref):
    parts = [_dot(o_ref[hd], wuv_ref[hd]) for hd in range(N_HEADS)]
    o = jnp.concatenate(parts, axis=-1).astype(BF16)
    y_ref[...] = x_ref[...] + _dot(o, wo_ref[...])


def _attn_out(o_lat, x, wuv, wo):
    m, d = x.shape
    tm = min(ROW_TILE, m)
    row = pl.BlockSpec((tm, d), lambda i: (i, 0))
    return pl.pallas_call(
        _attn_out_kernel,
        grid=(m // tm,),
        in_specs=[pl.BlockSpec((N_HEADS, tm, KV_LORA), lambda i: (0, i, 0)), row,
                  _full((N_HEADS, KV_LORA, V_HEAD_DIM)),
                  _full((N_HEADS * V_HEAD_DIM, d))],
        out_specs=row,
        out_shape=jax.ShapeDtypeStruct((m, d), F32),
        compiler_params=_params("arbitrary"),
        name="attn_out",
    )(o_lat, x, wuv, wo)


def _rope_tables(pos):
    inv = 1.0 / (ROPE_BASE ** (jnp.arange(0, QK_ROPE_DIM, 2, dtype=F32) / QK_ROPE_DIM))
    ang = pos.astype(F32)[:, None] * inv[None, :]
    cos, sin = jnp.cos(ang), jnp.sin(ang)
    return jnp.concatenate([cos, cos], axis=-1), jnp.concatenate([-sin, sin], axis=-1)


def _half_swap_cols(w, n_groups):
    k = w.shape[0]
    half = QK_ROPE_DIM // 2
    return w.reshape(k, n_groups, 2, half)[:, :, ::-1, :].reshape(k, n_groups * QK_ROPE_DIM)


def kernel(x_prompt, x_sample, state_conv, cache_ckv, cache_kpe, page_table, g_mix, g_ffn, w_up, w_down, w_pw1, b_pw1, w_dw, b_dw, g_cln, b_cln, w_pw2, b_pw2, g_kv_in, w_dkv, g_ckv, w_kr, w_uk, w_uv, w_dq, g_q, w_uq, w_qr, w_o, g_final):
    d = D_MODEL
    row = lambda v: v.reshape(1, -1)

    w1, b1 = w_pw1[0].astype(BF16), row(b_pw1[0])
    w2 = w_pw2[0].astype(BF16)
    wup, wdn = w_up.astype(BF16), w_down.astype(BF16)
    pad = jnp.zeros((d, LANES - QK_ROPE_DIM), F32)
    wkv = jnp.concatenate([w_dkv, w_kr, pad, _half_swap_cols(w_kr, 1), pad], axis=1).astype(BF16)
    wdq, wuq = w_dq[0].astype(BF16), w_uq[0].astype(BF16)
    wqr, wqs = w_qr[0].astype(BF16), _half_swap_cols(w_qr[0], N_HEADS).astype(BF16)
    wukt = jnp.transpose(w_uk, (1, 2, 0)).astype(BF16)
    wuv = jnp.transpose(w_uv, (1, 0, 2)).astype(BF16)
    wo = w_o[0].astype(BF16)

    def dense_front(x, state, bn, tt, cos, sin):
        n, t, _ = x.shape
        xf = x.reshape(n * t, d)
        u = _pw1_glu(xf, row(g_mix[0]), w1, b1)
        t_pad = max(t, tt)
        u3, x3 = u.reshape(n, t, d), x
        if t_pad != t:
            zpad = ((0, 0), (0, t_pad - t), (0, 0))
            u3, x3 = jnp.pad(u3, zpad), jnp.pad(x3, zpad)
        x1 = _conv_tail(u3, state, x3, w_dw[0], row(b_dw[0]), row(g_cln[0]), row(b_cln[0]),
                        w2, row(b_pw2[0]), bn, tt)[:, :t].reshape(n * t, d)
        x2 = _mlp(x1, row(g_ffn[0]), wup, wdn, 0, row(g_final), False)
        cos_h, sin_h = jnp.tile(cos, (1, N_HEADS)), jnp.tile(sin, (1, N_HEADS))
        ckv, kpe, kvb, q = _latent_qkv(x2, row(g_kv_in), wkv, row(g_ckv), cos, sin,
                                       row(g_mix[1]), wdq, row(g_q[0]), wuq, wqr, wqs, wukt,
                                       cos_h, sin_h)
        return u, x2, ckv, kpe, kvb, q

    def dense_back(o_lat, x2):
        x3 = _attn_out(o_lat, x2, wuv, wo)
        return _mlp(x3, row(g_ffn[1]), wup, wdn, 1, row(g_final), True)

    n_p, t_p, _ = x_prompt.shape
    cos_p, sin_p = _rope_tables(jnp.arange(t_p))
    u_p, x2_p, ckv_p, kpe_p, kvb_p, q_p = dense_front(x_prompt, None, 1, CONV_TIME_TILE, cos_p, sin_p)
    o_p = _prompt_attn(q_p, kvb_p, n_p, t_p)
    y_p = dense_back(o_p, x2_p).reshape(n_p, t_p, d)
    conv_p = u_p.reshape(n_p, t_p, d)[:, t_p - (CONV_WIDTH - 1):][None]

    n_s, t_s, _ = x_sample.shape
    past_len = page_table.shape[1] * cache_ckv.shape[1]
    cos_s, sin_s = _rope_tables(past_len + jnp.arange(t_s))
    cos_s, sin_s = jnp.tile(cos_s, (n_s, 1)), jnp.tile(sin_s, (n_s, 1))
    bn_s = CONV_TIME_TILE // SUBLANES
    u_s, x2_s, ckv_s, kpe_s, kvb_s, q_s = dense_front(x_sample, state_conv, bn_s, SUBLANES, cos_s, sin_s)
    rows_s = N_HEADS * t_s
    q_seq = q_s.reshape(N_HEADS, n_s, t_s, KV_DIM).transpose(1, 0, 2, 3).reshape(n_s, rows_s, KV_DIM)
    kv_new = jnp.pad(kvb_s.reshape(n_s, t_s, KV_DIM), ((0, 0), (0, NEW_ROWS_PAD - t_s), (0, 0)))
    o_seq = _sample_attn(page_table, q_seq, kv_new, cache_ckv, jnp.swapaxes(cache_kpe, 1, 2))
    o_s = o_seq.reshape(n_s, N_HEADS, t_s, KV_LORA).transpose(1, 0, 2, 3).reshape(N_HEADS, n_s * t_s, KV_LORA)
    y_s = dense_back(o_s, x2_s).reshape(n_s, t_s, d)
    conv_s = jnp.concatenate([state_conv[0][:, t_s:], u_s.reshape(n_s, t_s, d)], axis=1)[None]

    return (y_p, y_s, conv_p,
            ckv_p.reshape(n_p, t_p, KV_LORA), kpe_p.reshape(n_p, t_p, QK_ROPE_DIM),
            conv_s,
            ckv_s.reshape(n_s, t_s, KV_LORA), kpe_s.reshape(n_s, t_s, QK_ROPE_DIM))
```

```python
import functools

import jax
import jax.numpy as jnp
from jax import lax
from jax.experimental import pallas as pl
from jax.experimental.pallas import tpu as pltpu

D_MODEL = 1024
D_FF = 4 * D_MODEL
CONV_WIDTH = 31
N_HEADS = 8
QK_NOPE_DIM = 128
QK_ROPE_DIM = 64
V_HEAD_DIM = 128
Q_LORA = 768
KV_LORA = 256
KV_DIM = KV_LORA + QK_ROPE_DIM
ROPE_BASE = 10000.0
RMS_EPS = 1e-6
LN_EPS = 1e-5
ATTN_SCALE = (QK_NOPE_DIM + QK_ROPE_DIM) ** -0.5
EXP2_SCALE = ATTN_SCALE * 1.4426950408889634

LANES = 128
SUBLANES = 8
VMEM_LIMIT_BYTES = 56 * 1024 * 1024

ROW_TILE = 1024
MLP_ROW_TILE = 1024
FF_TILE = 1024
CONV_TIME_TILE = 512
CONV_TAIL = 32
CONV_ROW_CHUNK = 64
ATTN_TILE = 256

F32 = jnp.float32
BF16 = jnp.bfloat16


def _params(*semantics, flags=None):
    return pltpu.CompilerParams(dimension_semantics=semantics,
                                vmem_limit_bytes=VMEM_LIMIT_BYTES, flags=flags)


def _rms(x, g):
    return x * lax.rsqrt(jnp.mean(x * x, axis=-1, keepdims=True) + RMS_EPS) * g


def _dot(a, b):
    return jnp.dot(a, b, preferred_element_type=F32)


def _dot_nt(a, b):
    return lax.dot_general(a, b, (((1,), (1,)), ((), ())), preferred_element_type=F32)


def _full(shape):
    return pl.BlockSpec(shape, lambda *_: (0,) * len(shape))


def _pw1_glu_kernel(x_ref, g_ref, wa_ref, wb_ref, ba_ref, bb_ref, u_ref):
    hn = _rms(x_ref[...], g_ref[...]).astype(BF16)
    a = _dot(hn, wa_ref[...]) + ba_ref[...]
    b = _dot(hn, wb_ref[...]) + bb_ref[...]
    u_ref[...] = a * jax.nn.sigmoid(b)


def _pw1_glu(x, g, w, b):
    m, d = x.shape
    tm = min(ROW_TILE, m)
    row = pl.BlockSpec((tm, d), lambda i: (i, 0))
    half = lambda rows, h: pl.BlockSpec((rows, d), lambda i: (0, h))
    return pl.pallas_call(
        _pw1_glu_kernel,
        grid=(m // tm,),
        in_specs=[row, _full((1, d)), half(d, 0), half(d, 1), half(1, 0), half(1, 1)],
        out_specs=row,
        out_shape=jax.ShapeDtypeStruct((m, d), F32),
        compiler_params=_params("arbitrary"),
        name="pw1_glu",
    )(x, g, w, w, b, b)


def _conv_kernel(has_hist, u_ref, *refs):
    hist_ref = refs[0] if has_hist else None
    (x_ref, wdw_ref, bdw_ref, gln_ref, bln_ref, w2_ref, b2_ref,
     o_ref, win_ref, v_ref, slab_ref, shift_ref) = refs[1:] if has_hist else refs
    bn, tt, d = u_ref.shape
    rc = min(CONV_ROW_CHUNK, tt)
    n_row_chunks = tt // rc
    lead = CONV_TAIL - (CONV_WIDTH - 1)

    @pl.when(pl.program_id(1) == 0)
    def _():
        if has_hist:
            win_ref[:, 0:SUBLANES, :] = jnp.zeros((bn, SUBLANES, d), F32)
            win_ref[:, lead:CONV_TAIL, :] = hist_ref[...]
        else:
            win_ref[:, 0:CONV_TAIL, :] = jnp.zeros((bn, CONV_TAIL, d), F32)

    win_ref[:, CONV_TAIL:CONV_TAIL + tt, :] = u_ref[...]

    def chunk(idx, carry):
        b = idx // n_row_chunks
        r0 = pl.multiple_of((idx % n_row_chunks) * rc, SUBLANES)
        slab_ref[...] = win_ref[b, pl.ds(r0, rc + CONV_TAIL), :]
        for res in range(1, SUBLANES):
            shift_ref[res] = slab_ref[res:res + shift_ref.shape[1], :]
        for c in range(d // LANES):
            cols = slice(c * LANES, (c + 1) * LANES)
            acc = jnp.zeros((rc, LANES), F32)
            for k in range(CONV_WIDTH):
                res = (lead + k) % SUBLANES
                a = (lead + k) - res
                if res == 0:
                    xk = slab_ref[a:a + rc, cols]
                else:
                    xk = shift_ref[res, a:a + rc, cols]
                acc = acc + wdw_ref[k:k + 1, cols] * xk
            v_ref[b, pl.ds(r0, rc), cols] = acc
        return carry

    lax.fori_loop(0, bn * n_row_chunks, chunk, 0)

    win_ref[:, 0:CONV_TAIL, :] = win_ref[:, tt:tt + CONV_TAIL, :]

    v = v_ref[...].reshape(bn * tt, d) + bdw_ref[...]
    mu = jnp.mean(v, axis=-1, keepdims=True)
    vc = v - mu
    var = jnp.mean(vc * vc, axis=-1, keepdims=True)
    vn = vc * lax.rsqrt(var + LN_EPS) * gln_ref[...] + bln_ref[...]
    s = vn * jax.nn.sigmoid(vn)
    y = _dot(s.astype(BF16), w2_ref[...]) + b2_ref[...]
    o_ref[...] = x_ref[...] + y.reshape(bn, tt, d)


def _conv_tail(u, state, x, wdw, bdw, gln, bln, w2, b2, bn, tt):
    n, t, d = u.shape
    blk = pl.BlockSpec((bn, tt, d), lambda i, j: (i, j, 0))
    hist = [] if state is None else [state]
    hist_spec = [] if state is None else [
        pl.BlockSpec((None, bn, CONV_WIDTH - 1, d), lambda i, j: (0, i, 0, 0))]
    return pl.pallas_call(
        functools.partial(_conv_kernel, state is not None),
        grid=(n // bn, t // tt),
        in_specs=[blk] + hist_spec + [
                  blk,
                  _full((CONV_WIDTH, d)), _full((1, d)), _full((1, d)), _full((1, d)),
                  _full((d, d)), _full((1, d))],
        out_specs=blk,
        out_shape=jax.ShapeDtypeStruct((n, t, d), F32),
        scratch_shapes=[pltpu.VMEM((bn, CONV_TAIL + tt, d), F32),
                        pltpu.VMEM((bn, tt, d), F32),
                        pltpu.VMEM((min(CONV_ROW_CHUNK, tt) + CONV_TAIL, d), F32),
                        pltpu.VMEM((SUBLANES, min(CONV_ROW_CHUNK, tt) + CONV_TAIL - SUBLANES, d), F32)],
        compiler_params=_params("arbitrary", "arbitrary"),
        name="conv_tail",
    )(u, *hist, x, wdw, bdw, gln, bln, w2, b2)


def _mlp_kernel(final_norm, with_attn, *refs):
    if with_attn:
        olat_ref, wuv_ref, wo_ref = refs[:3]
        refs = refs[3:]
    x_ref, g_ref, wup_ref, wdn_ref, gf_ref, o_ref, hn_ref, acc_ref = refs
    f = pl.program_id(1)

    @pl.when(f == 0)
    def _():
        x = x_ref[...]
        if with_attn:
            parts = [_dot(olat_ref[hd], wuv_ref[hd]) for hd in range(N_HEADS)]
            x = x + _dot(jnp.concatenate(parts, axis=-1).astype(BF16), wo_ref[...])
        hn_ref[...] = _rms(x, g_ref[...]).astype(BF16)
        acc_ref[...] = x

    h = _dot(hn_ref[...], wup_ref[...])
    h = jnp.square(jnp.maximum(h, 0.0)).astype(BF16)
    acc_ref[...] += _dot(h, wdn_ref[...])

    @pl.when(f == pl.num_programs(1) - 1)
    def _():
        out = acc_ref[...]
        if final_norm:
            out = _rms(out, gf_ref[...])
        o_ref[...] = out


def _mlp(x, g, wup, wdn, layer, gf, final_norm, attn=None):
    m, d = x.shape
    tm = min(MLP_ROW_TILE, m)
    row = pl.BlockSpec((tm, d), lambda i, f: (i, 0))
    attn_args, attn_specs = [], []
    if attn is not None:
        attn_args = list(attn)
        attn_specs = [pl.BlockSpec((N_HEADS, tm, KV_LORA), lambda i, f: (0, i, 0)),
                      _full((N_HEADS, KV_LORA, V_HEAD_DIM)), _full((N_HEADS * V_HEAD_DIM, d))]
    return pl.pallas_call(
        functools.partial(_mlp_kernel, final_norm, attn is not None),
        grid=(m // tm, D_FF // FF_TILE),
        in_specs=attn_specs + [
                  row, _full((1, d)),
                  pl.BlockSpec((None, d, FF_TILE), lambda i, f: (layer, 0, f)),
                  pl.BlockSpec((None, FF_TILE, d), lambda i, f: (layer, f, 0)),
                  _full((1, d))],
        out_specs=row,
        out_shape=jax.ShapeDtypeStruct((m, d), F32),
        scratch_shapes=[pltpu.VMEM((tm, d), BF16), pltpu.VMEM((tm, d), F32)],
        compiler_params=_params("arbitrary", "arbitrary"),
        name="mlp_final" if final_norm else "mlp",
    )(*attn_args, x, g, wup, wdn, gf)


KV_W_COLS = 4 * LANES


def _latent_qkv_kernel(x_ref, gkv_ref, wkv_ref, gc_ref, cosk_ref, sink_ref,
                       gq_in_ref, wdq_ref, gq_ref, wuq_ref, wqr_ref, wqs_ref, wukt_ref,
                       cosq_ref, sinq_ref, ckv_ref, kpe_ref, kvb_ref, q_ref):
    x = x_ref[...]
    xn = x * lax.rsqrt(jnp.mean(x * x, axis=-1, keepdims=True) + RMS_EPS)

    a = _dot((xn * gkv_ref[...]).astype(BF16), wkv_ref[...])
    ckv = _rms(a[:, :KV_LORA], gc_ref[...])
    kr = a[:, KV_LORA:KV_LORA + QK_ROPE_DIM]
    ks = a[:, KV_LORA + LANES:KV_LORA + LANES + QK_ROPE_DIM]
    kpe = kr * cosk_ref[...] + ks * sink_ref[...]
    ckv_ref[...] = ckv
    kpe_ref[...] = kpe
    kvb_ref[:, :KV_LORA] = ckv.astype(BF16)
    kvb_ref[:, KV_LORA:] = kpe.astype(BF16)

    h = (xn * gq_in_ref[...]).astype(BF16)
    cq = _rms(_dot(h, wdq_ref[...]), gq_ref[...]).astype(BF16)
    qn = _dot(cq, wuq_ref[...]).astype(BF16)
    qpe = _dot(cq, wqr_ref[...]) * cosq_ref[...] + _dot(cq, wqs_ref[...]) * sinq_ref[...]
    for hd in range(N_HEADS):
        q_lat = _dot(qn[:, hd * QK_NOPE_DIM:(hd + 1) * QK_NOPE_DIM], wukt_ref[hd])
        q_ref[hd, :, :KV_LORA] = q_lat.astype(BF16)
        q_ref[hd, :, KV_LORA:] = qpe[:, hd * QK_ROPE_DIM:(hd + 1) * QK_ROPE_DIM].astype(BF16)


def _latent_qkv(x, gkv, wkv, gc, cosk, sink, gq_in, wdq, gq, wuq, wqr, wqs, wukt, cosq, sinq):
    m, d = x.shape
    tm = min(ROW_TILE, m)
    n_pos = cosk.shape[0] // tm
    rope_w = N_HEADS * QK_ROPE_DIM
    row = lambda w_: pl.BlockSpec((tm, w_), lambda i: (i, 0))
    pos = lambda w_: pl.BlockSpec((tm, w_), lambda i: (i % n_pos, 0))
    return pl.pallas_call(
        _latent_qkv_kernel,
        grid=(m // tm,),
        in_specs=[row(d), _full((1, d)), _full((d, KV_W_COLS)), _full((1, KV_LORA)),
                  pos(QK_ROPE_DIM), pos(QK_ROPE_DIM),
                  _full((1, d)), _full((d, Q_LORA)), _full((1, Q_LORA)),
                  _full((Q_LORA, N_HEADS * QK_NOPE_DIM)),
                  _full((Q_LORA, rope_w)), _full((Q_LORA, rope_w)),
                  _full((N_HEADS, QK_NOPE_DIM, KV_LORA)), pos(rope_w), pos(rope_w)],
        out_specs=[row(KV_LORA), row(QK_ROPE_DIM), row(KV_DIM),
                   pl.BlockSpec((N_HEADS, tm, KV_DIM), lambda i: (0, i, 0))],
        out_shape=[jax.ShapeDtypeStruct((m, KV_LORA), F32),
                   jax.ShapeDtypeStruct((m, QK_ROPE_DIM), F32),
                   jax.ShapeDtypeStruct((m, KV_DIM), BF16),
                   jax.ShapeDtypeStruct((N_HEADS, m, KV_DIM), BF16)],
        compiler_params=_params("arbitrary"),
        name="latent_qkv",
    )(x, gkv, wkv, gc, cosk, sink, gq_in, wdq, gq, wuq, wqr, wqs, wukt, cosq, sinq)


def _prompt_attn_kernel(q_ref, kv_ref, o_ref, s_ref, m_ref, mx_ref, l_ref, acc_ref):
    n_heads, tq, _ = q_ref.shape
    rows = n_heads * tq
    qi = pl.program_id(1)
    q_all = q_ref[...].reshape(rows, KV_DIM)

    def scores(j, mask):
        kj = pl.multiple_of(j * tq, tq)
        s = _dot_nt(q_all, kv_ref[pl.ds(kj, tq), :])
        if mask is not None:
            s = jnp.where(mask, s, -jnp.inf)
        s_ref[j] = s
        return s

    t_idx = lax.broadcasted_iota(jnp.int32, (n_heads, tq, tq), 1).reshape(rows, tq)
    c_idx = lax.broadcasted_iota(jnp.int32, (rows, tq), 1)
    m_ref[...] = scores(qi, c_idx <= t_idx)

    def score_tile(j, carry):
        m_ref[...] = jnp.maximum(m_ref[...], scores(j, None))
        return carry

    lax.fori_loop(0, qi, score_tile, 0)

    mx_ref[...] = jnp.broadcast_to(jnp.max(m_ref[...], axis=-1, keepdims=True), (rows, LANES))

    def probs(j):
        mx = mx_ref[...]
        p = jnp.exp2((s_ref[j] - jnp.concatenate([mx] * (tq // LANES), axis=1)) * EXP2_SCALE)
        kj = pl.multiple_of(j * tq, tq)
        pv = _dot(p.astype(BF16), kv_ref[pl.ds(kj, tq), :KV_LORA])
        return sum(p[:, i * LANES:(i + 1) * LANES] for i in range(tq // LANES)), pv

    l_ref[...], acc_ref[...] = probs(qi)

    def value_tile(j, carry):
        l_part, pv = probs(j)
        l_ref[...] += l_part
        acc_ref[...] += pv
        return carry

    lax.fori_loop(0, qi, value_tile, 0)

    inv_l = 1.0 / jnp.sum(l_ref[...], axis=-1, keepdims=True)
    o_ref[...] = (acc_ref[...] * inv_l).reshape(n_heads, tq, KV_LORA).astype(BF16)


def _prompt_attn(q, kvb, n, t):
    tq = ATTN_TILE
    nq = t // tq
    rows = N_HEADS * tq
    return pl.pallas_call(
        _prompt_attn_kernel,
        grid=(n, nq),
        in_specs=[pl.BlockSpec((N_HEADS, tq, KV_DIM), lambda b, i: (0, b * nq + i, 0)),
                  pl.BlockSpec((t, KV_DIM), lambda b, i: (b, 0))],
        out_specs=pl.BlockSpec((N_HEADS, tq, KV_LORA), lambda b, i: (0, b * nq + i, 0)),
        out_shape=jax.ShapeDtypeStruct((N_HEADS, n * t, KV_LORA), BF16),
        scratch_shapes=[pltpu.VMEM((nq, rows, tq), F32),
                        pltpu.VMEM((rows, tq), F32),
                        pltpu.VMEM((rows, LANES), F32),
                        pltpu.VMEM((rows, LANES), F32),
                        pltpu.VMEM((rows, KV_LORA), F32)],
        compiler_params=_params("arbitrary", "arbitrary"),
        name="prompt_attn",
    )(q, kvb)


NEW_ROWS_PAD = 16
CHUNK_PAGES = 32
RING_SLOTS = 4
DMA_UNROLL = 4


def _sample_attn_kernel(pt_ref, q_ref, new_ref, ckv_hbm, kpt_hbm, o_ref,
                        land_c, land_p, sem, kb_ref, kpt_ref, s_ref, p_ref):
    rows = q_ref.shape[1]
    n_new = rows // N_HEADS
    n_chunks = kb_ref.shape[0]
    page = land_c.shape[2]
    b = pl.program_id(0)
    total = pl.num_programs(0) * n_chunks
    ahead = RING_SLOTS - 1

    def page_copies(g, slot, i):
        pid = pt_ref[g * CHUNK_PAGES + i]
        return (pltpu.make_async_copy(ckv_hbm.at[pid], land_c.at[slot, i], sem.at[slot]),
                pltpu.make_async_copy(kpt_hbm.at[pid], land_p.at[slot, i], sem.at[slot]))

    def start_chunk(g):
        slot = g % RING_SLOTS

        def start_pair(j, carry):
            for parity in range(2):
                c_copy, p_copy = page_copies(g, slot, 2 * j + parity)
                c_copy.start(priority=parity)
                p_copy.start(priority=1 - parity)
            return carry

        lax.fori_loop(0, CHUNK_PAGES // 2, start_pair, 0, unroll=DMA_UNROLL // 2)

    def wait_chunk(g):
        slot = g % RING_SLOTS

        def wait_page(i, carry):
            for cp in page_copies(g, slot, i):
                cp.wait()
            return carry

        lax.fori_loop(0, CHUNK_PAGES, wait_page, 0, unroll=DMA_UNROLL)

    @pl.when(b == 0)
    def _():
        for g0 in range(ahead):
            start_chunk(g0)

    q = q_ref[0]
    q_lat, q_pe = q[:, :KV_LORA], q[:, KV_LORA:]

    def chunk_body(c, carry):
        g = b * n_chunks + c

        @pl.when(g + ahead < total)
        def _():
            start_chunk(g + ahead)

        wait_chunk(g)
        slot = g % RING_SLOTS
        for i in range(CHUNK_PAGES):
            kb_ref[c, i * page:(i + 1) * page, :] = land_c[slot, i].astype(BF16)
            kpt_ref[c, :, i * page:(i + 1) * page] = land_p[slot, i].astype(BF16)
        s_ref[c] = _dot_nt(q_lat, kb_ref[c]) + _dot(q_pe, kpt_ref[c])
        return carry

    lax.fori_loop(0, n_chunks, chunk_body, 0)

    kn = new_ref[0]
    s_new = _dot_nt(q, kn)
    t_idx = lax.broadcasted_iota(jnp.int32, s_new.shape, 0) % n_new
    c_idx = lax.broadcasted_iota(jnp.int32, s_new.shape, 1)
    s_new = jnp.where(c_idx <= t_idx, s_new, -jnp.inf)
    s_all = s_ref[...]
    m = jnp.maximum(jnp.max(jnp.max(s_all, axis=0), axis=-1, keepdims=True),
                    jnp.max(s_new, axis=-1, keepdims=True))
    p_all = jnp.exp2((s_all - m) * EXP2_SCALE)
    p_new = jnp.exp2((s_new - m) * EXP2_SCALE)
    l = (jnp.sum(jnp.sum(p_all, axis=0), axis=-1, keepdims=True)
         + jnp.sum(p_new, axis=-1, keepdims=True))
    p_ref[...] = p_all.astype(BF16)
    o = _dot(p_new.astype(BF16), kn[:, :KV_LORA])
    for c in range(n_chunks):
        o = o + _dot(p_ref[c], kb_ref[c])
    o_ref[0] = (o * (1.0 / l)).astype(BF16)


def _sample_attn(page_table, q, kv_new, cache_ckv, cache_kpt):
    n_seq, n_pages = page_table.shape
    page = cache_ckv.shape[1]
    rows = q.shape[1]
    n_chunks = n_pages // CHUNK_PAGES
    width = CHUNK_PAGES * page
    per_seq = lambda r, w: pl.BlockSpec((1, r, w), lambda b, pt: (b, 0, 0))
    grid_spec = pltpu.PrefetchScalarGridSpec(
        num_scalar_prefetch=1,
        grid=(n_seq,),
        in_specs=[per_seq(rows, KV_DIM), per_seq(NEW_ROWS_PAD, KV_DIM),
                  pl.BlockSpec(memory_space=pl.ANY), pl.BlockSpec(memory_space=pl.ANY)],
        out_specs=per_seq(rows, KV_LORA),
        scratch_shapes=[pltpu.VMEM((RING_SLOTS, CHUNK_PAGES, page, KV_LORA), F32),
                        pltpu.VMEM((RING_SLOTS, CHUNK_PAGES, QK_ROPE_DIM, page), F32),
                        pltpu.SemaphoreType.DMA((RING_SLOTS,)),
                        pltpu.VMEM((n_chunks, width, KV_LORA), BF16),
                        pltpu.VMEM((n_chunks, QK_ROPE_DIM, width), BF16),
                        pltpu.VMEM((n_chunks, rows, width), F32),
                        pltpu.VMEM((n_chunks, rows, width), BF16)])
    return pl.pallas_call(
        _sample_attn_kernel,
        grid_spec=grid_spec,
        out_shape=jax.ShapeDtypeStruct((n_seq, rows, KV_LORA), BF16),
        compiler_params=_params("arbitrary"),
        name="sample_attn",
    )(page_table.reshape(-1), q, kv_new, cache_ckv, cache_kpt)


def _rope_tables(pos):
    inv = 1.0 / (ROPE_BASE ** (jnp.arange(0, QK_ROPE_DIM, 2, dtype=F32) / QK_ROPE_DIM))
    ang = pos.astype(F32)[:, None] * inv[None, :]
    cos, sin = jnp.cos(ang), jnp.sin(ang)
    return jnp.concatenate([cos, cos], axis=-1), jnp.concatenate([-sin, sin], axis=-1)


def _half_swap_cols(w, n_groups):
    k = w.shape[0]
    half = QK_ROPE_DIM // 2
    return w.reshape(k, n_groups, 2, half)[:, :, ::-1, :].reshape(k, n_groups * QK_ROPE_DIM)


def kernel(x_prompt, x_sample, state_conv, cache_ckv, cache_kpe, page_table, g_mix, g_ffn, w_up, w_down, w_pw1, b_pw1, w_dw, b_dw, g_cln, b_cln, w_pw2, b_pw2, g_kv_in, w_dkv, g_ckv, w_kr, w_uk, w_uv, w_dq, g_q, w_uq, w_qr, w_o, g_final):
    d = D_MODEL
    row = lambda v: v.reshape(1, -1)

    w1, b1 = w_pw1[0].astype(BF16), row(b_pw1[0])
    w2 = w_pw2[0].astype(BF16)
    wup, wdn = w_up.astype(BF16), w_down.astype(BF16)
    pad = jnp.zeros((d, LANES - QK_ROPE_DIM), F32)
    wkv = jnp.concatenate([w_dkv, w_kr, pad, _half_swap_cols(w_kr, 1), pad], axis=1).astype(BF16)
    wdq, wuq = w_dq[0].astype(BF16), w_uq[0].astype(BF16)
    wqr, wqs = w_qr[0].astype(BF16), _half_swap_cols(w_qr[0], N_HEADS).astype(BF16)
    wukt = jnp.transpose(w_uk, (1, 2, 0)).astype(BF16)
    wuv = jnp.transpose(w_uv, (1, 0, 2)).astype(BF16)
    wo = w_o[0].astype(BF16)

    def dense_front(x, state, bn, tt, cos, sin):
        n, t, _ = x.shape
        xf = x.reshape(n * t, d)
        u = _pw1_glu(xf, row(g_mix[0]), w1, b1)
        t_pad = max(t, tt)
        u3, x3 = u.reshape(n, t, d), x
        if t_pad != t:
            zpad = ((0, 0), (0, t_pad - t), (0, 0))
            u3, x3 = jnp.pad(u3, zpad), jnp.pad(x3, zpad)
        x1 = _conv_tail(u3, state, x3, w_dw[0], row(b_dw[0]), row(g_cln[0]), row(b_cln[0]),
                        w2, row(b_pw2[0]), bn, tt)[:, :t].reshape(n * t, d)
        x2 = _mlp(x1, row(g_ffn[0]), wup, wdn, 0, row(g_final), False)
        cos_h, sin_h = jnp.tile(cos, (1, N_HEADS)), jnp.tile(sin, (1, N_HEADS))
        ckv, kpe, kvb, q = _latent_qkv(x2, row(g_kv_in), wkv, row(g_ckv), cos, sin,
                                       row(g_mix[1]), wdq, row(g_q[0]), wuq, wqr, wqs, wukt,
                                       cos_h, sin_h)
        return u, x2, ckv, kpe, kvb, q

    def dense_back(o_lat, x2):
        return _mlp(x2, row(g_ffn[1]), wup, wdn, 1, row(g_final), True, attn=(o_lat, wuv, wo))

    n_p, t_p, _ = x_prompt.shape
    cos_p, sin_p = _rope_tables(jnp.arange(t_p))
    u_p, x2_p, ckv_p, kpe_p, kvb_p, q_p = dense_front(x_prompt, None, 1, CONV_TIME_TILE, cos_p, sin_p)
    o_p = _prompt_attn(q_p, kvb_p, n_p, t_p)
    y_p = dense_back(o_p, x2_p).reshape(n_p, t_p, d)
    conv_p = u_p.reshape(n_p, t_p, d)[:, t_p - (CONV_WIDTH - 1):][None]

    n_s, t_s, _ = x_sample.shape
    past_len = page_table.shape[1] * cache_ckv.shape[1]
    cos_s, sin_s = _rope_tables(past_len + jnp.arange(t_s))
    cos_s, sin_s = jnp.tile(cos_s, (n_s, 1)), jnp.tile(sin_s, (n_s, 1))
    bn_s = CONV_TIME_TILE // SUBLANES
    u_s, x2_s, ckv_s, kpe_s, kvb_s, q_s = dense_front(x_sample, state_conv, bn_s, SUBLANES, cos_s, sin_s)
    rows_s = N_HEADS * t_s
    q_seq = q_s.reshape(N_HEADS, n_s, t_s, KV_DIM).transpose(1, 0, 2, 3).reshape(n_s, rows_s, KV_DIM)
    kv_new = jnp.pad(kvb_s.reshape(n_s, t_s, KV_DIM), ((0, 0), (0, NEW_ROWS_PAD - t_s), (0, 0)))
    o_seq = _sample_attn(page_table, q_seq, kv_new, cache_ckv, jnp.swapaxes(cache_kpe, 1, 2))
    o_s = o_seq.reshape(n_s, N_HEADS, t_s, KV_LORA).transpose(1, 0, 2, 3).reshape(N_HEADS, n_s * t_s, KV_LORA)
    y_s = dense_back(o_s, x2_s).reshape(n_s, t_s, d)
    conv_s = jnp.concatenate([state_conv[0][:, t_s:], u_s.reshape(n_s, t_s, d)], axis=1)[None]

    return (y_p, y_s, conv_p,
            ckv_p.reshape(n_p, t_p, KV_LORA), kpe_p.reshape(n_p, t_p, QK_ROPE_DIM),
            conv_s,
            ckv_s.reshape(n_s, t_s, KV_LORA), kpe_s.reshape(n_s, t_s, QK_ROPE_DIM))
```
